```python
import jax, jax.numpy as jnp
from jax import lax
import numpy as np

D_MODEL = 2048
BATCH = 2
SEQ = 4096
DEPTH = 2
DEC_BATCH = 8
DEC_SEQ = 64
PAST_LEN = 1024

CHUNK = 64
N_MIXERS = 2
N_GLA_LAYERS = (DEPTH + 1) // 2
N_CONV_LAYERS = DEPTH // 2
GLA_HEADS = 4
GLA_DK = D_MODEL // 2
GLA_DV = D_MODEL
GLA_DK_HEAD = GLA_DK // GLA_HEADS
GLA_DV_HEAD = GLA_DV // GLA_HEADS
GATE_RANK = 16
GATE_NORM = 16.0
GLA_IN = 2 * GLA_DK + 2 * GLA_DV + GATE_RANK
CONV_DIM = D_MODEL
CONV_WIDTH = 31
CONV_PAD = CONV_WIDTH - 1
CONV_IN = 3 * CONV_DIM
EPS = 1e-6

kernel_name = 'hybrid_gla_conformer_stream_step'


def _rmsnorm(x, g):
    x32 = x.astype(jnp.float32)
    y = x32 * lax.rsqrt(jnp.mean(x32 * x32, axis=-1, keepdims=True) + EPS)
    return (y * g.astype(jnp.float32)).astype(x.dtype)


def _layernorm(x, g, b):
    x32 = x.astype(jnp.float32)
    mu = jnp.mean(x32, axis=-1, keepdims=True)
    xc = x32 - mu
    y = xc * lax.rsqrt(jnp.mean(xc * xc, axis=-1, keepdims=True) + EPS)
    return (y * g.astype(jnp.float32) + b.astype(jnp.float32)).astype(x.dtype)


def _gla_recurrence(q, k, v, loga, s0):
    B, L, H, _ = q.shape
    dv = v.shape[-1]
    c = CHUNK if L >= CHUNK else L
    n = L // c

    def to_chunks(t):
        return t.reshape(B, n, c, H, t.shape[-1]).transpose(1, 0, 3, 2, 4)

    mask = jnp.tril(jnp.ones((c, c), dtype=bool))

    def step(S, inp):
        qc, kc, vc, ac = inp
        b = jnp.cumsum(ac, axis=-2)
        b_last = b[:, :, -1:, :]
        qe = qc * jnp.exp(b)
        ke = kc * jnp.exp(-b)
        att = jnp.where(mask, jnp.einsum('bhtd,bhsd->bhts', qe, ke), 0.0)
        o = jnp.einsum('bhts,bhsv->bhtv', att, vc) + jnp.einsum('bhtd,bhdv->bhtv', qe, S)
        kd = kc * jnp.exp(b_last - b)
        S = jnp.exp(b_last[:, :, 0, :])[..., None] * S + jnp.einsum('bhsd,bhsv->bhdv', kd, vc)
        return S, o

    S, o = lax.scan(step, s0, (to_chunks(q), to_chunks(k), to_chunks(v), to_chunks(loga)))
    o = o.transpose(1, 0, 3, 2, 4).reshape(B, L, H, dv)
    return o, S


def _gla_mixer(h, w_in, w_a2, b_a, g_norm, w_out, s0):
    B, L, _ = h.shape
    proj = h @ w_in
    q, k, v, r, a1 = jnp.split(
        proj, [GLA_DK, 2 * GLA_DK, 2 * GLA_DK + GLA_DV, 2 * GLA_DK + 2 * GLA_DV], axis=-1)
    loga = jax.nn.log_sigmoid((a1 @ w_a2 + b_a).astype(jnp.float32)) / GATE_NORM
    f32 = jnp.float32
    qh = q.astype(f32).reshape(B, L, GLA_HEADS, GLA_DK_HEAD) * (GLA_DK_HEAD ** -0.5)
    kh = k.astype(f32).reshape(B, L, GLA_HEADS, GLA_DK_HEAD)
    vh = v.astype(f32).reshape(B, L, GLA_HEADS, GLA_DV_HEAD)
    ah = loga.reshape(B, L, GLA_HEADS, GLA_DK_HEAD)
    o, s = _gla_recurrence(qh, kh, vh, ah, s0.astype(f32))
    o = o * lax.rsqrt(jnp.mean(o * o, axis=-1, keepdims=True) + EPS)
    o = o.reshape(B, L, GLA_DV) * g_norm.astype(f32)
    y = (o.astype(h.dtype) * jax.nn.silu(r)) @ w_out
    return y, s.astype(s0.dtype)


def _conv_mixer(h, w_in, b_in, conv_w, conv_b, ln_g, ln_b, w_out, buf):
    proj = h @ w_in + b_in
    a, ga, z = jnp.split(proj, [CONV_DIM, 2 * CONV_DIM], axis=-1)
    u = a * jax.nn.sigmoid(ga)
    u_ext = jnp.concatenate([buf.astype(u.dtype), u], axis=1)
    c = lax.conv_general_dilated(
        u_ext, conv_w[:, None, :], window_strides=(1,), padding='VALID',
        dimension_numbers=('NWC', 'WIO', 'NWC'), feature_group_count=CONV_DIM) + conv_b
    c = _layernorm(c, ln_g, ln_b)
    y = (jax.nn.silu(c) * jax.nn.silu(z)) @ w_out
    return y, u_ext[:, -CONV_PAD:, :]


def _trunk(x, gla_states, conv_bufs, norm_g, final_norm_g, gla_w_in, gla_w_a2, gla_b_a,
           gla_norm_g, gla_w_out, conv_w_in, conv_b_in, conv_w, conv_b, conv_ln_g,
           conv_ln_b, conv_w_out):
    new_gla, new_conv = [], []
    for i in range(DEPTH):
        h = _rmsnorm(x, norm_g[i])
        j = i // N_MIXERS
        if i % N_MIXERS == 0:
            y, s = _gla_mixer(h, gla_w_in[j], gla_w_a2[j], gla_b_a[j], gla_norm_g[j],
                              gla_w_out[j], gla_states[j])
            new_gla.append(s)
        else:
            y, s = _conv_mixer(h, conv_w_in[j], conv_b_in[j], conv_w[j], conv_b[j],
                               conv_ln_g[j], conv_ln_b[j], conv_w_out[j], conv_bufs[j])
            new_conv.append(s)
        x = x + y
    return _rmsnorm(x, final_norm_g), jnp.stack(new_gla), jnp.stack(new_conv)


def setup_inputs(seed: int = 0) -> dict:
    key = jax.random.key(seed)
    ks = jax.random.split(key, 20)
    n = jax.random.normal
    f = jnp.float32
    return {
        'x_prompt': n(ks[0], (BATCH, SEQ, D_MODEL), f),
        'x_sample': n(ks[1], (DEC_BATCH, DEC_SEQ, D_MODEL), f),
        'state_gla': 0.5 * n(ks[2], (N_GLA_LAYERS, DEC_BATCH, GLA_HEADS, GLA_DK_HEAD, GLA_DV_HEAD), f),
        'state_conv': 0.5 * n(ks[3], (N_CONV_LAYERS, DEC_BATCH, CONV_PAD, CONV_DIM), f),
        'norm_g': 1.0 + 0.02 * n(ks[4], (DEPTH, D_MODEL), f),
        'final_norm_g': 1.0 + 0.02 * n(ks[5], (D_MODEL,), f),
        'gla_w_in': n(ks[6], (N_GLA_LAYERS, D_MODEL, GLA_IN), f) * D_MODEL ** -0.5,
        'gla_w_a2': n(ks[7], (N_GLA_LAYERS, GATE_RANK, GLA_DK), f) * GATE_RANK ** -0.5,
        'gla_b_a': 0.1 * n(ks[8], (N_GLA_LAYERS, GLA_DK), f),
        'gla_norm_g': 1.0 + 0.02 * n(ks[9], (N_GLA_LAYERS, GLA_DV), f),
        'gla_w_out': n(ks[10], (N_GLA_LAYERS, GLA_DV, D_MODEL), f) * GLA_DV ** -0.5,
        'conv_w_in': n(ks[11], (N_CONV_LAYERS, D_MODEL, CONV_IN), f) * D_MODEL ** -0.5,
        'conv_b_in': 0.02 * n(ks[12], (N_CONV_LAYERS, CONV_IN), f),
        'conv_w': n(ks[13], (N_CONV_LAYERS, CONV_WIDTH, CONV_DIM), f) * CONV_WIDTH ** -0.5,
        'conv_b': 0.02 * n(ks[14], (N_CONV_LAYERS, CONV_DIM), f),
        'conv_ln_g': 1.0 + 0.02 * n(ks[15], (N_CONV_LAYERS, CONV_DIM), f),
        'conv_ln_b': 0.02 * n(ks[16], (N_CONV_LAYERS, CONV_DIM), f),
        'conv_w_out': n(ks[17], (N_CONV_LAYERS, CONV_DIM, D_MODEL), f) * CONV_DIM ** -0.5,
    }


def reference(x_prompt, x_sample, state_gla, state_conv, norm_g, final_norm_g, gla_w_in,
              gla_w_a2, gla_b_a, gla_norm_g, gla_w_out, conv_w_in, conv_b_in, conv_w, conv_b,
              conv_ln_g, conv_ln_b, conv_w_out):
    gla0 = jnp.zeros((N_GLA_LAYERS, x_prompt.shape[0], GLA_HEADS, GLA_DK_HEAD, GLA_DV_HEAD),
                     state_gla.dtype)
    conv0 = jnp.zeros((N_CONV_LAYERS, x_prompt.shape[0], CONV_PAD, CONV_DIM), state_conv.dtype)
    y_prompt, gla_p, conv_p = _trunk(
        x_prompt, gla0, conv0, norm_g, final_norm_g, gla_w_in, gla_w_a2, gla_b_a, gla_norm_g,
        gla_w_out, conv_w_in, conv_b_in, conv_w, conv_b, conv_ln_g, conv_ln_b, conv_w_out)
    y_sample, gla_s, conv_s = _trunk(
        x_sample, state_gla, state_conv, norm_g, final_norm_g, gla_w_in, gla_w_a2, gla_b_a,
        gla_norm_g, gla_w_out, conv_w_in, conv_b_in, conv_w, conv_b, conv_ln_g, conv_ln_b,
        conv_w_out)
    return (y_prompt, y_sample, gla_p, conv_p, gla_s, conv_s)
```

```python
import functools

import jax
import jax.numpy as jnp
from jax import lax
from jax.experimental import pallas as pl
from jax.experimental.pallas import tpu as pltpu

F32 = jnp.float32
BF16 = jnp.bfloat16

D_MODEL = 2048
CHUNK = 64
GLA_HEADS = 4
GLA_DK = 1024
GLA_DV = 2048
DK_HEAD = GLA_DK // GLA_HEADS
DV_HEAD = GLA_DV // GLA_HEADS
GATE_RANK = 16
GATE_NORM = 16.0
GLA_MAIN = 2 * GLA_DK + 2 * GLA_DV
CONV_DIM = 2048
CONV_WIDTH = 31
CONV_PAD = CONV_WIDTH - 1
EPS = 1e-6

SUBLANES = 8
LANES = 128
HALO = 32
VMEM_LIMIT = 56 * 1024 * 1024


def _cparams(sem):
    return pltpu.CompilerParams(dimension_semantics=sem, vmem_limit_bytes=VMEM_LIMIT)


def _silu(x):
    return x * jax.nn.sigmoid(x)


def _rmsnorm_rows(x, g):
    ms = jnp.mean(x * x, axis=-1, keepdims=True)
    return x * lax.rsqrt(ms + EPS) * g


def _gla_inproj_kernel(x_ref, g_ref, w_ref, wa1_ref, wa2_ref, ba_ref, o_ref, la_ref, h_ref):
    j = pl.program_id(1)

    @pl.when(j == 0)
    def _():
        h = _rmsnorm_rows(x_ref[...], g_ref[...]).astype(BF16)
        h_ref[...] = h
        a1 = jnp.dot(h, wa1_ref[...], preferred_element_type=F32)
        z = jnp.dot(a1.astype(BF16), wa2_ref[...], preferred_element_type=F32) + ba_ref[...]
        la_ref[...] = (jnp.minimum(z, 0.0) - jnp.log(1.0 + jnp.exp(-jnp.abs(z)))) * (1.0 / GATE_NORM)

    o_ref[...] = jnp.dot(h_ref[...], w_ref[...], preferred_element_type=F32).astype(BF16)


def _gla_inproj(x2d, g, w_main, wa1p, wa2p, ba, tm, tn):
    n_tok = x2d.shape[0]
    grid = (n_tok // tm, GLA_MAIN // tn)
    return pl.pallas_call(
        _gla_inproj_kernel,
        grid=grid,
        in_specs=[
            pl.BlockSpec((tm, D_MODEL), lambda i, j: (i, 0)),
            pl.BlockSpec((1, D_MODEL), lambda i, j: (0, 0)),
            pl.BlockSpec((D_MODEL, tn), lambda i, j: (0, j)),
            pl.BlockSpec((D_MODEL, LANES), lambda i, j: (0, 0)),
            pl.BlockSpec((LANES, GLA_DK), lambda i, j: (0, 0)),
            pl.BlockSpec((1, GLA_DK), lambda i, j: (0, 0)),
        ],
        out_specs=[
            pl.BlockSpec((tm, tn), lambda i, j: (i, j)),
            pl.BlockSpec((tm, GLA_DK), lambda i, j: (i, 0)),
        ],
        out_shape=[
            jax.ShapeDtypeStruct((n_tok, GLA_MAIN), BF16),
            jax.ShapeDtypeStruct((n_tok, GLA_DK), F32),
        ],
        scratch_shapes=[pltpu.VMEM((tm, D_MODEL), BF16)],
        compiler_params=_cparams(("arbitrary", "arbitrary")),
        name="gla_inproj",
    )(x2d, g, w_main, wa1p, wa2p, ba)


def _chunk_cumsum(la_ref, r0):
    sub = lax.broadcasted_iota(jnp.int32, (SUBLANES, GLA_DK), 0)
    groups = []
    carry = None
    for gi in range(CHUNK // SUBLANES):
        x = la_ref[0, pl.ds(r0 + gi * SUBLANES, SUBLANES), :]
        for s in (1, 2, 4):
            x = x + jnp.where(sub >= s, pltpu.roll(x, s, 0), 0.0)
        if carry is not None:
            x = x + carry
        carry = x[SUBLANES - 1:SUBLANES, :]
        groups.append(x)
    return jnp.concatenate(groups, axis=0)


def _gla_core_kernel(*refs, n_chunks, has_state):
    if has_state:
        q_ref, k_ref, v_ref, r_ref, la_ref, gn_ref, s0_ref, og_ref, so_ref, s_ref = refs
    else:
        q_ref, k_ref, v_ref, r_ref, la_ref, gn_ref, og_ref, so_ref, s_ref = refs
    t = pl.program_id(1)

    @pl.when(t == 0)
    def _():
        if has_state:
            s_ref[...] = s0_ref[0]
        else:
            s_ref[...] = jnp.zeros_like(s_ref)

    row = lax.broadcasted_iota(jnp.int32, (CHUNK, CHUNK), 0)
    col = lax.broadcasted_iota(jnp.int32, (CHUNK, CHUNK), 1)
    causal = row >= col

    def chunk_body(c, carry):
        r0 = pl.multiple_of(c * CHUNK, CHUNK)
        rows = pl.ds(r0, CHUNK)
        b = _chunk_cumsum(la_ref, r0)
        bl = b[CHUNK - 1:CHUNK, :]
        q = q_ref[0, rows, :].astype(F32)
        k = k_ref[0, rows, :].astype(F32)
        qe = (q * (jnp.exp(b) * (DK_HEAD ** -0.5))).astype(BF16)
        ke = (k * jnp.exp(-b)).astype(BF16)
        kd_t = jnp.transpose(k * jnp.exp(bl - b)).astype(BF16)
        dec_t = jnp.transpose(jnp.broadcast_to(jnp.exp(bl), (SUBLANES, GLA_DK)))
        for h in range(GLA_HEADS):
            ks = slice(h * DK_HEAD, (h + 1) * DK_HEAD)
            vs = slice(h * DV_HEAD, (h + 1) * DV_HEAD)
            qh = qe[:, ks]
            vh = v_ref[0, rows, vs]
            att = lax.dot_general(qh, ke[:, ks], (((1,), (1,)), ((), ())), preferred_element_type=F32)
            att = jnp.where(causal, att, 0.0).astype(BF16)
            s_old = s_ref[h]
            o = jnp.dot(att, vh, preferred_element_type=F32) + jnp.dot(
                qh, s_old.astype(BF16), preferred_element_type=F32)
            s_ref[h] = dec_t[ks, 0:1] * s_old + jnp.dot(kd_t[ks, :], vh, preferred_element_type=F32)
            on = _rmsnorm_rows(o, gn_ref[:, vs])
            gate = _silu(r_ref[0, rows, vs].astype(F32))
            og_ref[0, rows, vs] = (on * gate).astype(BF16)
        return carry

    lax.fori_loop(0, n_chunks, chunk_body, 0)

    @pl.when(t == pl.num_programs(1) - 1)
    def _():
        so_ref[0] = s_ref[...]


def _gla_core(qkvr, loga, gnorm, s0, tt):
    nb, seq, _ = qkvr.shape
    has_state = s0 is not None
    grid = (nb, seq // tt)
    in_specs = [
        pl.BlockSpec((1, tt, GLA_DK), lambda b, t: (b, t, 0)),
        pl.BlockSpec((1, tt, GLA_DK), lambda b, t: (b, t, 1)),
        pl.BlockSpec((1, tt, GLA_DV), lambda b, t: (b, t, 1)),
        pl.BlockSpec((1, tt, GLA_DV), lambda b, t: (b, t, 2)),
        pl.BlockSpec((1, tt, GLA_DK), lambda b, t: (b, t, 0)),
        pl.BlockSpec((1, GLA_DV), lambda b, t: (0, 0)),
    ]
    args = [qkvr, qkvr, qkvr, qkvr, loga, gnorm]
    state_spec = pl.BlockSpec((1, GLA_HEADS, DK_HEAD, DV_HEAD), lambda b, t: (b, 0, 0, 0))
    if has_state:
        in_specs.append(state_spec)
        args.append(s0)
    return pl.pallas_call(
        functools.partial(_gla_core_kernel, n_chunks=tt // CHUNK, has_state=has_state),
        grid=grid,
        in_specs=in_specs,
        out_specs=[pl.BlockSpec((1, tt, GLA_DV), lambda b, t: (b, t, 0)), state_spec],
        out_shape=[
            jax.ShapeDtypeStruct((nb, seq, GLA_DV), BF16),
            jax.ShapeDtypeStruct((nb, GLA_HEADS, DK_HEAD, DV_HEAD), F32),
        ],
        scratch_shapes=[pltpu.VMEM((GLA_HEADS, DK_HEAD, DV_HEAD), F32)],
        compiler_params=_cparams(("arbitrary", "arbitrary")),
        name="gla_core",
    )(*args)


def _outproj_kernel(*refs, ln_gate, final_norm, tn):
    if ln_gate:
        c_ref, sz_ref, lg_ref, lb_ref, w_ref, x_ref = refs[:6]
        rest = refs[6:]
    else:
        a_ref, w_ref, x_ref = refs[:3]
        rest = refs[3:]
    if final_norm:
        fg_ref, o_ref = rest[0], rest[1]
        rest = rest[2:]
    else:
        o_ref = rest[0]
        rest = rest[1:]

    if ln_gate:
        a_scr = rest[0]
        c = c_ref[...]
        mu = jnp.mean(c, axis=-1, keepdims=True)
        xc = c - mu
        y = xc * lax.rsqrt(jnp.mean(xc * xc, axis=-1, keepdims=True) + EPS) * lg_ref[...] + lb_ref[...]
        a_scr[...] = (_silu(y) * sz_ref[...].astype(F32)).astype(BF16)
        a_src = a_scr
    else:
        a_src = a_ref

    for n0 in range(0, D_MODEL, tn):
        cols = slice(n0, n0 + tn)
        y = jnp.dot(a_src[...], w_ref[:, cols], preferred_element_type=F32)
        o_ref[:, cols] = x_ref[:, cols] + y
    if final_norm:
        o_ref[...] = _rmsnorm_rows(o_ref[...], fg_ref[...])


def _outproj(act, w, x2d, tm, *, ln=None, final_g=None, tn=512):
    n_tok = x2d.shape[0]
    row_f32 = pl.BlockSpec((tm, D_MODEL), lambda i: (i, 0))
    vec = pl.BlockSpec((1, D_MODEL), lambda i: (0, 0))
    wspec = pl.BlockSpec((D_MODEL, D_MODEL), lambda i: (0, 0))
    in_specs, args, scratch = [], [], []
    if ln is not None:
        sz, lg, lb = ln
        in_specs += [row_f32, row_f32, vec, vec, wspec, row_f32]
        args += [act, sz, lg, lb, w, x2d]
        scratch.append(pltpu.VMEM((tm, D_MODEL), BF16))
    else:
        in_specs += [row_f32, wspec, row_f32]
        args += [act, w, x2d]
    if final_g is not None:
        in_specs.append(vec)
        args.append(final_g)
    return pl.pallas_call(
        functools.partial(_outproj_kernel, ln_gate=ln is not None, final_norm=final_g is not None, tn=tn),
        grid=(n_tok // tm,),
        in_specs=in_specs,
        out_specs=row_f32,
        out_shape=jax.ShapeDtypeStruct((n_tok, D_MODEL), F32),
        scratch_shapes=scratch,
        compiler_params=_cparams(("arbitrary",)),
        name="outproj_ln" if ln is not None else "outproj",
    )(*args)


def _conv_inproj_kernel(x_ref, g_ref, wa_ref, wg_ref, wz_ref, ba_ref, bg_ref, bz_ref, u_ref, sz_ref, h_ref):
    j = pl.program_id(1)

    @pl.when(j == 0)
    def _():
        h_ref[...] = _rmsnorm_rows(x_ref[...], g_ref[...]).astype(BF16)

    h = h_ref[...]
    a = jnp.dot(h, wa_ref[...], preferred_element_type=F32) + ba_ref[...]
    ga = jnp.dot(h, wg_ref[...], preferred_element_type=F32) + bg_ref[...]
    u_ref[...] = a * jax.nn.sigmoid(ga)
    z = jnp.dot(h, wz_ref[...], preferred_element_type=F32) + bz_ref[...]
    sz_ref[...] = _silu(z).astype(BF16)


def _conv_inproj(x2d, g, w, b, tm, tn):
    n_tok = x2d.shape[0]
    nj = CONV_DIM // tn
    grid = (n_tok // tm, nj)
    wspec = lambda off: pl.BlockSpec((D_MODEL, tn), lambda i, j: (0, j + off * nj))
    bspec = lambda off: pl.BlockSpec((1, tn), lambda i, j: (0, j + off * nj))
    return pl.pallas_call(
        _conv_inproj_kernel,
        grid=grid,
        in_specs=[
            pl.BlockSpec((tm, D_MODEL), lambda i, j: (i, 0)),
            pl.BlockSpec((1, D_MODEL), lambda i, j: (0, 0)),
            wspec(0), wspec(1), wspec(2),
            bspec(0), bspec(1), bspec(2),
        ],
        out_specs=[
            pl.BlockSpec((tm, tn), lambda i, j: (i, j)),
            pl.BlockSpec((tm, tn), lambda i, j: (i, j)),
        ],
        out_shape=[
            jax.ShapeDtypeStruct((n_tok, CONV_DIM), F32),
            jax.ShapeDtypeStruct((n_tok, CONV_DIM), BF16),
        ],
        scratch_shapes=[pltpu.VMEM((tm, D_MODEL), BF16)],
        compiler_params=_cparams(("arbitrary", "arbitrary")),
        name="conv_inproj",
    )(x2d, g, w, w, w, b, b, b)


def _dwconv_kernel(*refs, tt, ct, rb, has_state):
    if has_state:
        u_ref, w_ref, b_ref, st_ref, c_ref, so_ref, buf = refs
    else:
        u_ref, w_ref, b_ref, c_ref, so_ref, buf = refs
    t = pl.program_id(2)

    @pl.when(t == 0)
    def _():
        if has_state:
            buf[0:HALO, :] = st_ref[0]
        else:
            buf[0:HALO, :] = jnp.zeros((HALO, ct), F32)

    buf[HALO:HALO + tt, :] = u_ref[0]
    first = HALO - CONV_PAD
    for l0 in range(0, ct, LANES):
        lanes = slice(l0, l0 + LANES)
        for r0 in range(0, tt, rb):
            acc = jnp.broadcast_to(b_ref[:, lanes], (rb, LANES))
            for kk in range(CONV_WIDTH):
                acc = acc + buf[first + r0 + kk:first + r0 + kk + rb, lanes] * w_ref[kk:kk + 1, lanes]
            c_ref[0, r0:r0 + rb, lanes] = acc

    @pl.when(t == pl.num_programs(2) - 1)
    def _():
        so_ref[0] = buf[tt + first:tt + HALO, :]

    buf[0:HALO, :] = buf[tt:tt + HALO, :]


def _dwconv(u, w, b, st, tt, ct=512, rb=64):
    nb, seq, _ = u.shape
    has_state = st is not None
    grid = (nb, CONV_DIM // ct, seq // tt)
    in_specs = [
        pl.BlockSpec((1, tt, ct), lambda bi, ci, ti: (bi, ti, ci)),
        pl.BlockSpec((CONV_WIDTH, ct), lambda bi, ci, ti: (0, ci)),
        pl.BlockSpec((1, ct), lambda bi, ci, ti: (0, ci)),
    ]
    args = [u, w, b]
    if has_state:
        in_specs.append(pl.BlockSpec((1, HALO, ct), lambda bi, ci, ti: (bi, 0, ci)))
        args.append(st)
    return pl.pallas_call(
        functools.partial(_dwconv_kernel, tt=tt, ct=ct, rb=rb, has_state=has_state),
        grid=grid,
        in_specs=in_specs,
        out_specs=[
            pl.BlockSpec((1, tt, ct), lambda bi, ci, ti: (bi, ti, ci)),
            pl.BlockSpec((1, CONV_PAD, ct), lambda bi, ci, ti: (bi, 0, ci)),
        ],
        out_shape=[
            jax.ShapeDtypeStruct((nb, seq, CONV_DIM), F32),
            jax.ShapeDtypeStruct((nb, CONV_PAD, CONV_DIM), F32),
        ],
        scratch_shapes=[pltpu.VMEM((HALO + tt, ct), F32)],
        compiler_params=_cparams(("arbitrary", "arbitrary", "arbitrary")),
        name="dwconv",
    )(*args)


def _trunk(x, gla_state, conv_state, p, tm, tm_out, tt_gla, tt_conv):
    nb, seq, _ = x.shape
    n_tok = nb * seq
    x2d = x.reshape(n_tok, D_MODEL)

    qkvr, loga = _gla_inproj(x2d, p["norm_g0"], p["gla_w_main"], p["gla_wa1"], p["gla_wa2"], p["gla_ba"], tm, 512)
    og, s_new = _gla_core(qkvr.reshape(nb, seq, GLA_MAIN), loga.reshape(nb, seq, GLA_DK), p["gla_gn"],
                          gla_state, tt_gla)
    x1 = _outproj(og.reshape(n_tok, GLA_DV), p["gla_w_out"], x2d, tm_out)

    u, sz = _conv_inproj(x1, p["norm_g1"], p["conv_w_in"], p["conv_b_in"], tm, 512)
    c, conv_new = _dwconv(u.reshape(nb, seq, CONV_DIM), p["conv_w"], p["conv_b"], conv_state, tt_conv)
    y = _outproj(c.reshape(n_tok, CONV_DIM), p["conv_w_out"], x1, tm_out,
                 ln=(sz, p["conv_ln_g"], p["conv_ln_b"]), final_g=p["final_g"])
    return y.reshape(nb, seq, D_MODEL), s_new[None], conv_new[None]


def _prepare(norm_g, final_norm_g, gla_w_in, gla_w_a2, gla_b_a, gla_norm_g, gla_w_out, conv_w_in, conv_b_in,
             conv_w, conv_b, conv_ln_g, conv_ln_b, conv_w_out):
    w_in0 = gla_w_in[0]
    return {
        "norm_g0": norm_g[0:1],
        "norm_g1": norm_g[1:2],
        "final_g": final_norm_g[None, :],
        "gla_w_main": w_in0[:, :GLA_MAIN].astype(BF16),
        "gla_wa1": jnp.pad(w_in0[:, GLA_MAIN:], ((0, 0), (0, LANES - GATE_RANK))).astype(BF16),
        "gla_wa2": jnp.pad(gla_w_a2[0], ((0, LANES - GATE_RANK), (0, 0))).astype(BF16),
        "gla_ba": gla_b_a[0:1],
        "gla_gn": gla_norm_g[0:1],
        "gla_w_out": gla_w_out[0].astype(BF16),
        "conv_w_in": conv_w_in[0].astype(BF16),
        "conv_b_in": conv_b_in[0:1],
        "conv_w": conv_w[0],
        "conv_b": conv_b[0:1],
        "conv_ln_g": conv_ln_g[0:1],
        "conv_ln_b": conv_ln_b[0:1],
        "conv_w_out": conv_w_out[0].astype(BF16),
    }


def kernel(x_prompt, x_sample, state_gla, state_conv, norm_g, final_norm_g, gla_w_in, gla_w_a2, gla_b_a,
           gla_norm_g, gla_w_out, conv_w_in, conv_b_in, conv_w, conv_b, conv_ln_g, conv_ln_b, conv_w_out):
    p = _prepare(norm_g, final_norm_g, gla_w_in, gla_w_a2, gla_b_a, gla_norm_g, gla_w_out, conv_w_in, conv_b_in,
                 conv_w, conv_b, conv_ln_g, conv_ln_b, conv_w_out)
    y_p, gla_p, conv_p = _trunk(x_prompt, None, None, p, tm=1024, tm_out=512, tt_gla=512, tt_conv=256)
    conv_hist = jnp.pad(state_conv[0], ((0, 0), (HALO - CONV_PAD, 0), (0, 0)))
    y_s, gla_s, conv_s = _trunk(x_sample, state_gla[0], conv_hist, p, tm=512, tm_out=512, tt_gla=64, tt_conv=64)
    return (y_p, y_s, gla_p, conv_p, gla_s, conv_s)
```

```python
import functools

import jax
import jax.numpy as jnp
from jax import lax
from jax.experimental import pallas as pl
from jax.experimental.pallas import tpu as pltpu

F32 = jnp.float32
BF16 = jnp.bfloat16

D_MODEL = 2048
CHUNK = 64
GLA_HEADS = 4
GLA_DK = 1024
GLA_DV = 2048
DK_HEAD = GLA_DK // GLA_HEADS
DV_HEAD = GLA_DV // GLA_HEADS
GATE_RANK = 16
GATE_NORM = 16.0
GLA_MAIN = 2 * GLA_DK + 2 * GLA_DV
CONV_DIM = 2048
CONV_WIDTH = 31
CONV_PAD = CONV_WIDTH - 1
EPS = 1e-6

SUBLANES = 8
LANES = 128
HALO = 32
ROW_STRIDE = 2
VMEM_LIMIT = 56 * 1024 * 1024


def _cparams(sem):
    return pltpu.CompilerParams(dimension_semantics=sem, vmem_limit_bytes=VMEM_LIMIT)


def _silu(x):
    return x * jax.nn.sigmoid(x)


def _rmsnorm_rows(x, g):
    ms = jnp.mean(x * x, axis=-1, keepdims=True)
    return x * lax.rsqrt(ms + EPS) * g


def _gla_inproj_kernel(x_ref, g_ref, w_ref, wa1_ref, wa2_ref, ba_ref, o_ref, la_ref, h_ref):
    j = pl.program_id(1)

    @pl.when(j == 0)
    def _():
        h = _rmsnorm_rows(x_ref[...], g_ref[...]).astype(BF16)
        h_ref[...] = h
        a1 = jnp.dot(h, wa1_ref[...], preferred_element_type=F32)
        z = jnp.dot(a1.astype(BF16), wa2_ref[...], preferred_element_type=F32) + ba_ref[...]
        la_ref[...] = (jnp.minimum(z, 0.0) - jnp.log(1.0 + jnp.exp(-jnp.abs(z)))) * (1.0 / GATE_NORM)

    o_ref[...] = jnp.dot(h_ref[...], w_ref[...], preferred_element_type=F32).astype(BF16)


def _gla_inproj(x2d, g, w_main, wa1p, wa2p, ba, tm, tn):
    n_tok = x2d.shape[0]
    grid = (n_tok // tm, GLA_MAIN // tn)
    return pl.pallas_call(
        _gla_inproj_kernel,
        grid=grid,
        in_specs=[
            pl.BlockSpec((tm, D_MODEL), lambda i, j: (i, 0)),
            pl.BlockSpec((1, D_MODEL), lambda i, j: (0, 0)),
            pl.BlockSpec((D_MODEL, tn), lambda i, j: (0, j)),
            pl.BlockSpec((D_MODEL, LANES), lambda i, j: (0, 0)),
            pl.BlockSpec((LANES, GLA_DK), lambda i, j: (0, 0)),
            pl.BlockSpec((1, GLA_DK), lambda i, j: (0, 0)),
        ],
        out_specs=[
            pl.BlockSpec((tm, tn), lambda i, j: (i, j)),
            pl.BlockSpec((tm, GLA_DK), lambda i, j: (i, 0)),
        ],
        out_shape=[
            jax.ShapeDtypeStruct((n_tok, GLA_MAIN), BF16),
            jax.ShapeDtypeStruct((n_tok, GLA_DK), F32),
        ],
        scratch_shapes=[pltpu.VMEM((tm, D_MODEL), BF16)],
        compiler_params=_cparams(("arbitrary", "arbitrary")),
        name="gla_inproj",
    )(x2d, g, w_main, wa1p, wa2p, ba)


def _chunk_cumsum(la_ref, r0):
    sub = lax.broadcasted_iota(jnp.int32, (SUBLANES, GLA_DK), 0)
    groups = []
    carry = None
    for gi in range(CHUNK // SUBLANES):
        x = la_ref[0, pl.ds(r0 + gi * SUBLANES, SUBLANES), :]
        for s in (1, 2, 4):
            x = x + jnp.where(sub >= s, pltpu.roll(x, s, 0), 0.0)
        if carry is not None:
            x = x + carry
        carry = x[SUBLANES - 1:SUBLANES, :]
        groups.append(x)
    return jnp.concatenate(groups, axis=0)


def _gla_core_kernel(*refs, n_chunks, has_state):
    if has_state:
        q_ref, k_ref, v_ref, r_ref, la_ref, gn_ref, s0_ref, og_ref, so_ref, s_ref = refs
    else:
        q_ref, k_ref, v_ref, r_ref, la_ref, gn_ref, og_ref, so_ref, s_ref = refs
    t = pl.program_id(1)

    @pl.when(t == 0)
    def _():
        if has_state:
            s_ref[...] = s0_ref[0]
        else:
            s_ref[...] = jnp.zeros_like(s_ref)

    row = lax.broadcasted_iota(jnp.int32, (CHUNK, CHUNK), 0)
    col = lax.broadcasted_iota(jnp.int32, (CHUNK, CHUNK), 1)
    causal = row >= col

    def chunk_body(c, carry):
        r0 = pl.multiple_of(c * CHUNK, CHUNK)
        rows = pl.ds(r0, CHUNK)
        b = _chunk_cumsum(la_ref, r0)
        bl = b[CHUNK - 1:CHUNK, :]
        q = q_ref[0, rows, :].astype(F32)
        k = k_ref[0, rows, :].astype(F32)
        qe = (q * (jnp.exp(b) * (DK_HEAD ** -0.5))).astype(BF16)
        ke = (k * jnp.exp(-b)).astype(BF16)
        kd_t = jnp.transpose(k * jnp.exp(bl - b)).astype(BF16)
        dec_t = jnp.transpose(jnp.broadcast_to(jnp.exp(bl), (SUBLANES, GLA_DK)))
        for h in range(GLA_HEADS):
            ks = slice(h * DK_HEAD, (h + 1) * DK_HEAD)
            vs = slice(h * DV_HEAD, (h + 1) * DV_HEAD)
            qh = qe[:, ks]
            vh = v_ref[0, rows, vs]
            att = lax.dot_general(qh, ke[:, ks], (((1,), (1,)), ((), ())), preferred_element_type=F32)
            att = jnp.where(causal, att, 0.0).astype(BF16)
            s_old = s_ref[h]
            o = jnp.dot(att, vh, preferred_element_type=F32) + jnp.dot(
                qh, s_old.astype(BF16), preferred_element_type=F32)
            s_ref[h] = dec_t[ks, 0:1] * s_old + jnp.dot(kd_t[ks, :], vh, preferred_element_type=F32)
            on = _rmsnorm_rows(o, gn_ref[:, vs])
            gate = _silu(r_ref[0, rows, vs].astype(F32))
            og_ref[0, rows, vs] = (on * gate).astype(BF16)
        return carry

    lax.fori_loop(0, n_chunks, chunk_body, 0)

    @pl.when(t == pl.num_programs(1) - 1)
    def _():
        so_ref[0] = s_ref[...]


def _gla_core(qkvr, loga, gnorm, s0, tt):
    nb, seq, _ = qkvr.shape
    has_state = s0 is not None
    grid = (nb, seq // tt)
    in_specs = [
        pl.BlockSpec((1, tt, GLA_DK), lambda b, t: (b, t, 0)),
        pl.BlockSpec((1, tt, GLA_DK), lambda b, t: (b, t, 1)),
        pl.BlockSpec((1, tt, GLA_DV), lambda b, t: (b, t, 1)),
        pl.BlockSpec((1, tt, GLA_DV), lambda b, t: (b, t, 2)),
        pl.BlockSpec((1, tt, GLA_DK), lambda b, t: (b, t, 0)),
        pl.BlockSpec((1, GLA_DV), lambda b, t: (0, 0)),
    ]
    args = [qkvr, qkvr, qkvr, qkvr, loga, gnorm]
    state_spec = pl.BlockSpec((1, GLA_HEADS, DK_HEAD, DV_HEAD), lambda b, t: (b, 0, 0, 0))
    if has_state:
        in_specs.append(state_spec)
        args.append(s0)
    return pl.pallas_call(
        functools.partial(_gla_core_kernel, n_chunks=tt // CHUNK, has_state=has_state),
        grid=grid,
        in_specs=in_specs,
        out_specs=[pl.BlockSpec((1, tt, GLA_DV), lambda b, t: (b, t, 0)), state_spec],
        out_shape=[
            jax.ShapeDtypeStruct((nb, seq, GLA_DV), BF16),
            jax.ShapeDtypeStruct((nb, GLA_HEADS, DK_HEAD, DV_HEAD), F32),
        ],
        scratch_shapes=[pltpu.VMEM((GLA_HEADS, DK_HEAD, DV_HEAD), F32)],
        compiler_params=_cparams(("arbitrary", "arbitrary")),
        name="gla_core",
    )(*args)


def _outproj_kernel(*refs, ln_gate, final_norm, tn):
    if ln_gate:
        c_ref, sz_ref, lg_ref, lb_ref, w_ref, x_ref = refs[:6]
        rest = refs[6:]
    else:
        a_ref, w_ref, x_ref = refs[:3]
        rest = refs[3:]
    if final_norm:
        fg_ref, o_ref = rest[0], rest[1]
        rest = rest[2:]
    else:
        o_ref = rest[0]
        rest = rest[1:]

    if ln_gate:
        a_scr = rest[0]
        c = c_ref[...]
        mu = jnp.mean(c, axis=-1, keepdims=True)
        xc = c - mu
        y = xc * lax.rsqrt(jnp.mean(xc * xc, axis=-1, keepdims=True) + EPS) * lg_ref[...] + lb_ref[...]
        a_scr[...] = (_silu(y) * sz_ref[...].astype(F32)).astype(BF16)
        a_src = a_scr
    else:
        a_src = a_ref

    for n0 in range(0, D_MODEL, tn):
        cols = slice(n0, n0 + tn)
        y = jnp.dot(a_src[...], w_ref[:, cols], preferred_element_type=F32)
        o_ref[:, cols] = x_ref[:, cols] + y
    if final_norm:
        o_ref[...] = _rmsnorm_rows(o_ref[...], fg_ref[...])


def _outproj(act, w, x2d, tm, *, ln=None, final_g=None, tn=512):
    n_tok = x2d.shape[0]
    row_f32 = pl.BlockSpec((tm, D_MODEL), lambda i: (i, 0))
    vec = pl.BlockSpec((1, D_MODEL), lambda i: (0, 0))
    wspec = pl.BlockSpec((D_MODEL, D_MODEL), lambda i: (0, 0))
    in_specs, args, scratch = [], [], []
    if ln is not None:
        sz, lg, lb = ln
        in_specs += [row_f32, row_f32, vec, vec, wspec, row_f32]
        args += [act, sz, lg, lb, w, x2d]
        scratch.append(pltpu.VMEM((tm, D_MODEL), BF16))
    else:
        in_specs += [row_f32, wspec, row_f32]
        args += [act, w, x2d]
    if final_g is not None:
        in_specs.append(vec)
        args.append(final_g)
    return pl.pallas_call(
        functools.partial(_outproj_kernel, ln_gate=ln is not None, final_norm=final_g is not None, tn=tn),
        grid=(n_tok // tm,),
        in_specs=in_specs,
        out_specs=row_f32,
        out_shape=jax.ShapeDtypeStruct((n_tok, D_MODEL), F32),
        scratch_shapes=scratch,
        compiler_params=_cparams(("arbitrary",)),
        name="outproj_ln" if ln is not None else "outproj",
    )(*args)


def _conv_inproj_kernel(x_ref, g_ref, wa_ref, wg_ref, wz_ref, ba_ref, bg_ref, bz_ref, u_ref, sz_ref, h_ref):
    j = pl.program_id(1)

    @pl.when(j == 0)
    def _():
        h_ref[...] = _rmsnorm_rows(x_ref[...], g_ref[...]).astype(BF16)

    h = h_ref[...]
    a = jnp.dot(h, wa_ref[...], preferred_element_type=F32) + ba_ref[...]
    ga = jnp.dot(h, wg_ref[...], preferred_element_type=F32) + bg_ref[...]
    u_ref[...] = a * jax.nn.sigmoid(ga)
    z = jnp.dot(h, wz_ref[...], preferred_element_type=F32) + bz_ref[...]
    sz_ref[...] = _silu(z).astype(BF16)


def _conv_inproj(x2d, g, w, b, tm, tn):
    n_tok = x2d.shape[0]
    nj = CONV_DIM // tn
    grid = (n_tok // tm, nj)
    wspec = lambda off: pl.BlockSpec((D_MODEL, tn), lambda i, j: (0, j + off * nj))
    bspec = lambda off: pl.BlockSpec((1, tn), lambda i, j: (0, j + off * nj))
    return pl.pallas_call(
        _conv_inproj_kernel,
        grid=grid,
        in_specs=[
            pl.BlockSpec((tm, D_MODEL), lambda i, j: (i, 0)),
            pl.BlockSpec((1, D_MODEL), lambda i, j: (0, 0)),
            wspec(0), wspec(1), wspec(2),
            bspec(0), bspec(1), bspec(2),
        ],
        out_specs=[
            pl.BlockSpec((tm, tn), lambda i, j: (i, j)),
            pl.BlockSpec((tm, tn), lambda i, j: (i, j)),
        ],
        out_shape=[
            jax.ShapeDtypeStruct((n_tok, CONV_DIM), F32),
            jax.ShapeDtypeStruct((n_tok, CONV_DIM), BF16),
        ],
        scratch_shapes=[pltpu.VMEM((tm, D_MODEL), BF16)],
        compiler_params=_cparams(("arbitrary", "arbitrary")),
        name="conv_inproj",
    )(x2d, g, w, w, w, b, b, b)


def _dwconv_kernel(*refs, tt, ct, rb, has_state):
    if has_state:
        u_ref, w_ref, b_ref, st_ref, c_ref, so_ref, buf = refs
    else:
        u_ref, w_ref, b_ref, c_ref, so_ref, buf = refs
    t = pl.program_id(2)

    def trows(start, n):
        return pl.ds(ROW_STRIDE * start, n, stride=ROW_STRIDE)

    first = HALO - CONV_PAD
    for s in range(ct // LANES):
        lanes = slice(s * LANES, (s + 1) * LANES)

        @pl.when(t == 0)
        def _():
            if has_state:
                buf[s, trows(0, HALO), :] = st_ref[0, :, lanes]
            else:
                buf[s, trows(0, HALO), :] = jnp.zeros((HALO, LANES), F32)

        buf[s, trows(HALO, tt), :] = u_ref[0, :, lanes]
        for r0 in range(0, tt, rb):
            acc = jnp.broadcast_to(b_ref[:, lanes], (rb, LANES))
            for kk in range(CONV_WIDTH):
                acc = acc + buf[s, trows(first + r0 + kk, rb), :] * w_ref[kk:kk + 1, lanes]
            c_ref[0, r0:r0 + rb, lanes] = acc

        hist = buf[s, trows(tt, HALO), :]
        so_ref[0, :, lanes] = hist
        buf[s, trows(0, HALO), :] = hist


def _dwconv(u, w, b, st, tt, ct=512, rb=64):
    nb, seq, _ = u.shape
    has_state = st is not None
    grid = (nb, CONV_DIM // ct, seq // tt)
    in_specs = [
        pl.BlockSpec((1, tt, ct), lambda bi, ci, ti: (bi, ti, ci)),
        pl.BlockSpec((CONV_WIDTH, ct), lambda bi, ci, ti: (0, ci)),
        pl.BlockSpec((1, ct), lambda bi, ci, ti: (0, ci)),
    ]
    args = [u, w, b]
    if has_state:
        in_specs.append(pl.BlockSpec((1, HALO, ct), lambda bi, ci, ti: (bi, 0, ci)))
        args.append(st)
    return pl.pallas_call(
        functools.partial(_dwconv_kernel, tt=tt, ct=ct, rb=rb, has_state=has_state),
        grid=grid,
        in_specs=in_specs,
        out_specs=[
            pl.BlockSpec((1, tt, ct), lambda bi, ci, ti: (bi, ti, ci)),
            pl.BlockSpec((1, HALO, ct), lambda bi, ci, ti: (bi, 0, ci)),
        ],
        out_shape=[
            jax.ShapeDtypeStruct((nb, seq, CONV_DIM), F32),
            jax.ShapeDtypeStruct((nb, HALO, CONV_DIM), F32),
        ],
        scratch_shapes=[pltpu.VMEM((ct // LANES, ROW_STRIDE * (HALO + tt), LANES), F32)],
        compiler_params=_cparams(("arbitrary", "arbitrary", "arbitrary")),
        name="dwconv",
    )(*args)


def _trunk(x, gla_state, conv_state, p, tm, tm_out, tt_gla, tt_conv):
    nb, seq, _ = x.shape
    n_tok = nb * seq
    x2d = x.reshape(n_tok, D_MODEL)

    qkvr, loga = _gla_inproj(x2d, p["norm_g0"], p["gla_w_main"], p["gla_wa1"], p["gla_wa2"], p["gla_ba"], tm, 512)
    og, s_new = _gla_core(qkvr.reshape(nb, seq, GLA_MAIN), loga.reshape(nb, seq, GLA_DK), p["gla_gn"],
                          gla_state, tt_gla)
    x1 = _outproj(og.reshape(n_tok, GLA_DV), p["gla_w_out"], x2d, tm_out)

    u, sz = _conv_inproj(x1, p["norm_g1"], p["conv_w_in"], p["conv_b_in"], tm, 512)
    c, conv_new = _dwconv(u.reshape(nb, seq, CONV_DIM), p["conv_w"], p["conv_b"], conv_state, tt_conv)
    y = _outproj(c.reshape(n_tok, CONV_DIM), p["conv_w_out"], x1, tm_out,
                 ln=(sz, p["conv_ln_g"], p["conv_ln_b"]), final_g=p["final_g"])
    return y.reshape(nb, seq, D_MODEL), s_new[None], conv_new[None, :, HALO - CONV_PAD:, :]


def _prepare(norm_g, final_norm_g, gla_w_in, gla_w_a2, gla_b_a, gla_norm_g, gla_w_out, conv_w_in, conv_b_in,
             conv_w, conv_b, conv_ln_g, conv_ln_b, conv_w_out):
    w_in0 = gla_w_in[0]
    return {
        "norm_g0": norm_g[0:1],
        "norm_g1": norm_g[1:2],
        "final_g": final_norm_g[None, :],
        "gla_w_main": w_in0[:, :GLA_MAIN].astype(BF16),
        "gla_wa1": jnp.pad(w_in0[:, GLA_MAIN:], ((0, 0), (0, LANES - GATE_RANK))).astype(BF16),
        "gla_wa2": jnp.pad(gla_w_a2[0], ((0, LANES - GATE_RANK), (0, 0))).astype(BF16),
        "gla_ba": gla_b_a[0:1],
        "gla_gn": gla_norm_g[0:1],
        "gla_w_out": gla_w_out[0].astype(BF16),
        "conv_w_in": conv_w_in[0].astype(BF16),
        "conv_b_in": conv_b_in[0:1],
        "conv_w": conv_w[0],
        "conv_b": conv_b[0:1],
        "conv_ln_g": conv_ln_g[0:1],
        "conv_ln_b": conv_ln_b[0:1],
        "conv_w_out": conv_w_out[0].astype(BF16),
    }


def kernel(x_prompt, x_sample, state_gla, state_conv, norm_g, final_norm_g, gla_w_in, gla_w_a2, gla_b_a,
           gla_norm_g, gla_w_out, conv_w_in, conv_b_in, conv_w, conv_b, conv_ln_g, conv_ln_b, conv_w_out):
    p = _prepare(norm_g, final_norm_g, gla_w_in, gla_w_a2, gla_b_a, gla_norm_g, gla_w_out, conv_w_in, conv_b_in,
                 conv_w, conv_b, conv_ln_g, conv_ln_b, conv_w_out)
    y_p, gla_p, conv_p = _trunk(x_prompt, None, None, p, tm=1024, tm_out=512, tt_gla=512, tt_conv=256)
    conv_hist = jnp.pad(state_conv[0], ((0, 0), (HALO - CONV_PAD, 0), (0, 0)))
    y_s, gla_s, conv_s = _trunk(x_sample, state_gla[0], conv_hist, p, tm=512, tm_out=512, tt_gla=64, tt_conv=64)
    return (y_p, y_s, gla_p, conv_p, gla_s, conv_s)
```

```python
import functools

import jax
import jax.numpy as jnp
from jax import lax
from jax.experimental import pallas as pl
from jax.experimental.pallas import tpu as pltpu

F32 = jnp.float32
BF16 = jnp.bfloat16

D_MODEL = 2048
CHUNK = 64
GLA_HEADS = 4
GLA_DK = 1024
GLA_DV = 2048
DK_HEAD = GLA_DK // GLA_HEADS
DV_HEAD = GLA_DV // GLA_HEADS
GATE_RANK = 16
GATE_NORM = 16.0
GLA_MAIN = 2 * GLA_DK + 2 * GLA_DV
CONV_DIM = 2048
CONV_WIDTH = 31
CONV_PAD = CONV_WIDTH - 1
EPS = 1e-6

SUBLANES = 8
LANES = 128
HALO = 32
ROW_STRIDE = 2
VMEM_LIMIT = 56 * 1024 * 1024


def _cparams(sem):
    return pltpu.CompilerParams(dimension_semantics=sem, vmem_limit_bytes=VMEM_LIMIT)


def _after(x, dep):
    half = jnp.uint32(16)
    zero = lax.shift_right_logical(lax.shift_right_logical(lax.bitcast_convert_type(dep, jnp.uint32), half), half)
    return lax.bitcast_convert_type(lax.bitcast_convert_type(x, jnp.uint32) | zero, F32)


def _silu(x):
    return x * jax.nn.sigmoid(x)


def _rmsnorm_rows(x, g):
    ms = jnp.mean(x * x, axis=-1, keepdims=True)
    return x * lax.rsqrt(ms + EPS) * g


def _resident(shape):
    return pl.BlockSpec(shape, lambda *_: (0,) * len(shape), pipeline_mode=pl.Buffered(1))


def _lagged(tm, width, lag, n_tiles):
    return pl.BlockSpec((tm, width), lambda s: (jnp.clip(s - lag, 0, n_tiles - 1), 0))


def _gla_front_kernel(x_ref, g_ref, w_ref, wa1_ref, wa2_ref, ba_ref, o_ref, la_ref, h_cur, h_nxt, *, tn):
    s = pl.program_id(0)

    @pl.when(s == 0)
    def _():
        h_nxt[...] = jnp.zeros_like(h_nxt)

    h_cur[...] = h_nxt[...]

    h = h_cur[...]
    a1 = jnp.dot(h, wa1_ref[...], preferred_element_type=F32)
    z = jnp.dot(a1.astype(BF16), wa2_ref[...], preferred_element_type=F32) + ba_ref[...]
    la_ref[...] = (jnp.minimum(z, 0.0) - jnp.log(1.0 + jnp.exp(-jnp.abs(z)))) * (1.0 / GATE_NORM)
    for n0 in range(0, GLA_MAIN, tn):
        cols = slice(n0, n0 + tn)
        o_ref[:, cols] = jnp.dot(h_cur[...], w_ref[:, cols], preferred_element_type=F32).astype(BF16)

    h_nxt[...] = _rmsnorm_rows(x_ref[...], g_ref[...]).astype(BF16)


def _gla_front(x2d, g, w_main, wa1p, wa2p, ba, tm, tn=512):
    n_tok = x2d.shape[0]
    n_tiles = n_tok // tm
    return pl.pallas_call(
        functools.partial(_gla_front_kernel, tn=tn),
        grid=(n_tiles + 1,),
        in_specs=[
            _lagged(tm, D_MODEL, 0, n_tiles),
            _resident((1, D_MODEL)),
            _resident((D_MODEL, GLA_MAIN)),
            _resident((D_MODEL, LANES)),
            _resident((LANES, GLA_DK)),
            _resident((1, GLA_DK)),
        ],
        out_specs=[
            _lagged(tm, GLA_MAIN, 1, n_tiles),
            _lagged(tm, GLA_DK, 1, n_tiles),
        ],
        out_shape=[
            jax.ShapeDtypeStruct((n_tok, GLA_MAIN), BF16),
            jax.ShapeDtypeStruct((n_tok, GLA_DK), F32),
        ],
        scratch_shapes=[pltpu.VMEM((tm, D_MODEL), BF16), pltpu.VMEM((tm, D_MODEL), BF16)],
        compiler_params=_cparams(("arbitrary",)),
        name="gla_front",
    )(x2d, g, w_main, wa1p, wa2p, ba)


def _chunk_cumsum(la_ref, r0):
    sub = lax.broadcasted_iota(jnp.int32, (SUBLANES, GLA_DK), 0)
    groups = []
    carry = None
    for gi in range(CHUNK // SUBLANES):
        x = la_ref[0, pl.ds(r0 + gi * SUBLANES, SUBLANES), :]
        for s in (1, 2, 4):
            x = x + jnp.where(sub >= s, pltpu.roll(x, s, 0), 0.0)
        if carry is not None:
            x = x + carry
        carry = x[SUBLANES - 1:SUBLANES, :]
        groups.append(x)
    return jnp.concatenate(groups, axis=0)


def _gla_core_kernel(*refs, n_chunks, has_state):
    if has_state:
        q_ref, k_ref, v_ref, r_ref, la_ref, gn_ref, s0_ref, og_ref, so_ref, s_ref = refs
    else:
        q_ref, k_ref, v_ref, r_ref, la_ref, gn_ref, og_ref, so_ref, s_ref = refs
    t = pl.program_id(1)

    @pl.when(t == 0)
    def _():
        if has_state:
            s_ref[...] = s0_ref[0]
        else:
            s_ref[...] = jnp.zeros_like(s_ref)

    row = lax.broadcasted_iota(jnp.int32, (CHUNK, CHUNK), 0)
    col = lax.broadcasted_iota(jnp.int32, (CHUNK, CHUNK), 1)
    causal = row >= col

    def chunk_body(c, carry):
        r0 = pl.multiple_of(c * CHUNK, CHUNK)
        rows = pl.ds(r0, CHUNK)
        b = _chunk_cumsum(la_ref, r0)
        bl = b[CHUNK - 1:CHUNK, :]
        q = q_ref[0, rows, :].astype(F32)
        k = k_ref[0, rows, :].astype(F32)
        qe = (q * (jnp.exp(b) * (DK_HEAD ** -0.5))).astype(BF16)
        ke = (k * jnp.exp(-b)).astype(BF16)
        kd_t = jnp.transpose(k * jnp.exp(bl - b)).astype(BF16)
        dec_t = jnp.transpose(jnp.broadcast_to(jnp.exp(bl), (SUBLANES, GLA_DK)))
        for h in range(GLA_HEADS):
            ks = slice(h * DK_HEAD, (h + 1) * DK_HEAD)
            vs = slice(h * DV_HEAD, (h + 1) * DV_HEAD)
            qh = qe[:, ks]
            vh = v_ref[0, rows, vs]
            att = lax.dot_general(qh, ke[:, ks], (((1,), (1,)), ((), ())), preferred_element_type=F32)
            att = jnp.where(causal, att, 0.0).astype(BF16)
            s_old = s_ref[h]
            o = jnp.dot(att, vh, preferred_element_type=F32) + jnp.dot(
                qh, s_old.astype(BF16), preferred_element_type=F32)
            s_ref[h] = dec_t[ks, 0:1] * s_old + jnp.dot(kd_t[ks, :], vh, preferred_element_type=F32)
            on = _rmsnorm_rows(o, gn_ref[:, vs])
            gate = _silu(r_ref[0, rows, vs].astype(F32))
            og_ref[0, rows, vs] = (on * gate).astype(BF16)
        return carry

    lax.fori_loop(0, n_chunks, chunk_body, 0)

    @pl.when(t == pl.num_programs(1) - 1)
    def _():
        so_ref[0] = s_ref[...]


def _gla_core(qkvr, loga, gnorm, s0, tt):
    nb, seq, _ = qkvr.shape
    has_state = s0 is not None
    grid = (nb, seq // tt)
    in_specs = [
        pl.BlockSpec((1, tt, GLA_DK), lambda b, t: (b, t, 0)),
        pl.BlockSpec((1, tt, GLA_DK), lambda b, t: (b, t, 1)),
        pl.BlockSpec((1, tt, GLA_DV), lambda b, t: (b, t, 1)),
        pl.BlockSpec((1, tt, GLA_DV), lambda b, t: (b, t, 2)),
        pl.BlockSpec((1, tt, GLA_DK), lambda b, t: (b, t, 0)),
        pl.BlockSpec((1, GLA_DV), lambda b, t: (0, 0)),
    ]
    args = [qkvr, qkvr, qkvr, qkvr, loga, gnorm]
    state_spec = pl.BlockSpec((1, GLA_HEADS, DK_HEAD, DV_HEAD), lambda b, t: (b, 0, 0, 0))
    if has_state:
        in_specs.append(state_spec)
        args.append(s0)
    return pl.pallas_call(
        functools.partial(_gla_core_kernel, n_chunks=tt // CHUNK, has_state=has_state),
        grid=grid,
        in_specs=in_specs,
        out_specs=[pl.BlockSpec((1, tt, GLA_DV), lambda b, t: (b, t, 0)), state_spec],
        out_shape=[
            jax.ShapeDtypeStruct((nb, seq, GLA_DV), BF16),
            jax.ShapeDtypeStruct((nb, GLA_HEADS, DK_HEAD, DV_HEAD), F32),
        ],
        scratch_shapes=[pltpu.VMEM((GLA_HEADS, DK_HEAD, DV_HEAD), F32)],
        compiler_params=_cparams(("arbitrary", "arbitrary")),
        name="gla_core",
    )(*args)


def _outproj_kernel(*refs, ln_gate, final_norm, tn):
    if ln_gate:
        c_ref, sz_ref, lg_ref, lb_ref, w_ref, x_ref = refs[:6]
        rest = refs[6:]
    else:
        a_ref, w_ref, x_ref = refs[:3]
        rest = refs[3:]
    if final_norm:
        fg_ref, o_ref = rest[0], rest[1]
        rest = rest[2:]
    else:
        o_ref = rest[0]
        rest = rest[1:]

    if ln_gate:
        a_scr = rest[0]
        c = c_ref[...]
        mu = jnp.mean(c, axis=-1, keepdims=True)
        xc = c - mu
        y = xc * lax.rsqrt(jnp.mean(xc * xc, axis=-1, keepdims=True) + EPS) * lg_ref[...] + lb_ref[...]
        a_scr[...] = (_silu(y) * sz_ref[...].astype(F32)).astype(BF16)
        a_src = a_scr
    else:
        a_src = a_ref

    for n0 in range(0, D_MODEL, tn):
        cols = slice(n0, n0 + tn)
        y = jnp.dot(a_src[...], w_ref[:, cols], preferred_element_type=F32)
        o_ref[:, cols] = x_ref[:, cols] + y
    if final_norm:
        o_ref[...] = _rmsnorm_rows(o_ref[...], fg_ref[...])


def _outproj(act, w, x2d, tm, *, ln=None, final_g=None, tn=512):
    n_tok = x2d.shape[0]
    rows = pl.BlockSpec((tm, D_MODEL), lambda i: (i, 0))
    vec = _resident((1, D_MODEL))
    wspec = _resident((D_MODEL, D_MODEL))
    in_specs, args, scratch = [], [], []
    if ln is not None:
        sz, lg, lb = ln
        in_specs += [rows, rows, vec, vec, wspec, rows]
        args += [act, sz, lg, lb, w, x2d]
        scratch.append(pltpu.VMEM((tm, D_MODEL), BF16))
    else:
        in_specs += [rows, wspec, rows]
        args += [act, w, x2d]
    if final_g is not None:
        in_specs.append(vec)
        args.append(final_g)
    return pl.pallas_call(
        functools.partial(_outproj_kernel, ln_gate=ln is not None, final_norm=final_g is not None, tn=tn),
        grid=(n_tok // tm,),
        in_specs=in_specs,
        out_specs=rows,
        out_shape=jax.ShapeDtypeStruct((n_tok, D_MODEL), F32),
        scratch_shapes=scratch,
        compiler_params=_cparams(("arbitrary",)),
        name="outproj_ln" if ln is not None else "outproj",
    )(*args)


def _time_rows(start, n):
    return pl.ds(ROW_STRIDE * start, n, stride=ROW_STRIDE)


def _conv_front_kernel(*refs, n_tiles, tm, seg, tiles_per_seq, has_state, tn, rb):
    if has_state:
        (x_ref, g_ref, w_ref, b_ref, cw_ref, cb_ref, st_ref,
         sz_ref, c_ref, so_ref, h_cur, h_nxt, u_stage, buf, hist) = refs
    else:
        (x_ref, g_ref, w_ref, b_ref, cw_ref, cb_ref,
         sz_ref, c_ref, so_ref, h_cur, h_nxt, u_stage, buf, hist) = refs
    s = pl.program_id(0)
    nseg = tm // seg
    region = HALO + seg
    n_slab = CONV_DIM // LANES
    first = HALO - CONV_PAD

    @pl.when(s == 0)
    def _():
        h_nxt[...] = jnp.zeros_like(h_nxt)
        u_stage[...] = jnp.zeros_like(u_stage)
        hist[...] = jnp.zeros_like(hist)

    h_cur[...] = h_nxt[...]
    t_conv = jnp.maximum(s - 2, 0)
    for g in range(nseg):
        if nseg > 1:
            seq_i = t_conv * nseg + g
            hist_g = st_ref[seq_i] if has_state else jnp.zeros((HALO, CONV_DIM), F32)
        else:
            seq_i = t_conv // tiles_per_seq

            @pl.when(lax.rem(t_conv, tiles_per_seq) == 0)
            def _():
                hist[...] = st_ref[seq_i] if has_state else jnp.zeros_like(hist)

            hist_g = hist[...]
        for sl in range(n_slab):
            lanes = slice(sl * LANES, (sl + 1) * LANES)
            buf[sl, _time_rows(g * region, HALO), :] = hist_g[:, lanes]
            buf[sl, _time_rows(g * region + HALO, seg), :] = u_stage[g * seg:(g + 1) * seg, lanes]

    for g in range(nseg):
        tail = u_stage[(g + 1) * seg - HALO:(g + 1) * seg, :]
        if nseg > 1:
            so_ref[t_conv * nseg + g] = tail
        else:
            hist[...] = tail

            @pl.when(lax.rem(t_conv, tiles_per_seq) == tiles_per_seq - 1)
            def _():
                so_ref[t_conv // tiles_per_seq] = tail

    n_chunks = CONV_DIM // tn
    slabs_per_chunk = n_slab // n_chunks
    prev_acc = None
    for j in range(n_chunks):
        n0 = j * tn
        cols = slice(n0, n0 + tn)
        a = jnp.dot(h_cur[...], w_ref[:, cols], preferred_element_type=F32) + b_ref[:, cols]
        gcols = slice(CONV_DIM + n0, CONV_DIM + n0 + tn)
        ga = jnp.dot(h_cur[...], w_ref[:, gcols], preferred_element_type=F32) + b_ref[:, gcols]
        u_stage[:, cols] = a * jax.nn.sigmoid(ga)
        zcols = slice(2 * CONV_DIM + n0, 2 * CONV_DIM + n0 + tn)
        z = jnp.dot(h_cur[...], w_ref[:, zcols], preferred_element_type=F32) + b_ref[:, zcols]
        sz_ref[:, cols] = _silu(z).astype(BF16)

        for sl in range(j * slabs_per_chunk, (j + 1) * slabs_per_chunk):
            lanes = slice(sl * LANES, (sl + 1) * LANES)
            for g in range(nseg):
                for r0 in range(0, seg, rb):
                    acc = jnp.broadcast_to(cb_ref[:, lanes], (rb, LANES))
                    if prev_acc is not None:
                        acc = _after(acc, prev_acc)
                    for kk in range(CONV_WIDTH):
                        acc = acc + (buf[sl, _time_rows(g * region + first + r0 + kk, rb), :]
                                     * cw_ref[kk:kk + 1, lanes])
                    c_ref[g * seg + r0:g * seg + r0 + rb, lanes] = acc
                    prev_acc = acc

    h_nxt[...] = _rmsnorm_rows(x_ref[...], g_ref[...]).astype(BF16)


def _conv_front(x2d, g, w, b, cw, cb, st, n_seq, tm, tn=512, rb=64):
    n_tok = x2d.shape[0]
    seq_len = n_tok // n_seq
    seg = min(tm, seq_len)
    n_tiles = n_tok // tm
    has_state = st is not None
    in_specs = [
        _lagged(tm, D_MODEL, 0, n_tiles),
        _resident((1, D_MODEL)),
        _resident((D_MODEL, 3 * CONV_DIM)),
        _resident((1, 3 * CONV_DIM)),
        _resident((CONV_WIDTH, CONV_DIM)),
        _resident((1, CONV_DIM)),
    ]
    args = [x2d, g, w, b, cw, cb]
    if has_state:
        in_specs.append(_resident((n_seq, HALO, CONV_DIM)))
        args.append(st)
    return pl.pallas_call(
        functools.partial(_conv_front_kernel, n_tiles=n_tiles, tm=tm, seg=seg, tiles_per_seq=seq_len // seg,
                          has_state=has_state, tn=tn, rb=rb),
        grid=(n_tiles + 2,),
        in_specs=in_specs,
        out_specs=[
            _lagged(tm, CONV_DIM, 1, n_tiles),
            _lagged(tm, CONV_DIM, 2, n_tiles),
            pl.BlockSpec((n_seq, HALO, CONV_DIM), lambda s: (0, 0, 0)),
        ],
        out_shape=[
            jax.ShapeDtypeStruct((n_tok, CONV_DIM), BF16),
            jax.ShapeDtypeStruct((n_tok, CONV_DIM), F32),
            jax.ShapeDtypeStruct((n_seq, HALO, CONV_DIM), F32),
        ],
        scratch_shapes=[
            pltpu.VMEM((tm, D_MODEL), BF16),
            pltpu.VMEM((tm, D_MODEL), BF16),
            pltpu.VMEM((tm, CONV_DIM), F32),
            pltpu.VMEM((CONV_DIM // LANES, ROW_STRIDE * (tm // seg) * (HALO + seg), LANES), F32),
            pltpu.VMEM((HALO, CONV_DIM), F32),
        ],
        compiler_params=_cparams(("arbitrary",)),
        name="conv_front",
    )(*args)


def _trunk(x, gla_state, conv_state, p, tm, tm_out, tt_gla):
    nb, seq, _ = x.shape
    n_tok = nb * seq
    x2d = x.reshape(n_tok, D_MODEL)

    qkvr, loga = _gla_front(x2d, p["norm_g0"], p["gla_w_main"], p["gla_wa1"], p["gla_wa2"], p["gla_ba"], tm)
    og, s_new = _gla_core(qkvr.reshape(nb, seq, GLA_MAIN), loga.reshape(nb, seq, GLA_DK), p["gla_gn"],
                          gla_state, tt_gla)
    x1 = _outproj(og.reshape(n_tok, GLA_DV), p["gla_w_out"], x2d, tm_out)

    sz, c, conv_new = _conv_front(x1, p["norm_g1"], p["conv_w_in"], p["conv_b_in"], p["conv_w"], p["conv_b"],
                                  conv_state, nb, tm)
    y = _outproj(c, p["conv_w_out"], x1, tm_out, ln=(sz, p["conv_ln_g"], p["conv_ln_b"]), final_g=p["final_g"])
    return y.reshape(nb, seq, D_MODEL), s_new[None], conv_new[None, :, HALO - CONV_PAD:, :]


def _prepare(norm_g, final_norm_g, gla_w_in, gla_w_a2, gla_b_a, gla_norm_g, gla_w_out, conv_w_in, conv_b_in,
             conv_w, conv_b, conv_ln_g, conv_ln_b, conv_w_out):
    w_in0 = gla_w_in[0]
    return {
        "norm_g0": norm_g[0:1],
        "norm_g1": norm_g[1:2],
        "final_g": final_norm_g[None, :],
        "gla_w_main": w_in0[:, :GLA_MAIN].astype(BF16),
        "gla_wa1": jnp.pad(w_in0[:, GLA_MAIN:], ((0, 0), (0, LANES - GATE_RANK))).astype(BF16),
        "gla_wa2": jnp.pad(gla_w_a2[0], ((0, LANES - GATE_RANK), (0, 0))).astype(BF16),
        "gla_ba": gla_b_a[0:1],
        "gla_gn": gla_norm_g[0:1],
        "gla_w_out": gla_w_out[0].astype(BF16),
        "conv_w_in": conv_w_in[0].astype(BF16),
        "conv_b_in": conv_b_in[0:1],
        "conv_w": conv_w[0],
        "conv_b": conv_b[0:1],
        "conv_ln_g": conv_ln_g[0:1],
        "conv_ln_b": conv_ln_b[0:1],
        "conv_w_out": conv_w_out[0].astype(BF16),
    }


def kernel(x_prompt, x_sample, state_gla, state_conv, norm_g, final_norm_g, gla_w_in, gla_w_a2, gla_b_a,
           gla_norm_g, gla_w_out, conv_w_in, conv_b_in, conv_w, conv_b, conv_ln_g, conv_ln_b, conv_w_out):
    p = _prepare(norm_g, final_norm_g, gla_w_in, gla_w_a2, gla_b_a, gla_norm_g, gla_w_out, conv_w_in, conv_b_in,
                 conv_w, conv_b, conv_ln_g, conv_ln_b, conv_w_out)
    y_p, gla_p, conv_p = _trunk(x_prompt, None, None, p, tm=256, tm_out=512, tt_gla=512)
    conv_hist = jnp.pad(state_conv[0], ((0, 0), (HALO - CONV_PAD, 0), (0, 0)))
    y_s, gla_s, conv_s = _trunk(x_sample, state_gla[0], conv_hist, p, tm=256, tm_out=512, tt_gla=64)
    return (y_p, y_s, gla_p, conv_p, gla_s, conv_s)
```

```python
import functools

import jax
import jax.numpy as jnp
from jax import lax
from jax.experimental import pallas as pl
from jax.experimental.pallas import tpu as pltpu

F32 = jnp.float32
BF16 = jnp.bfloat16

D_MODEL = 2048
CHUNK = 64
GLA_HEADS = 4
GLA_DK = 1024
GLA_DV = 2048
DK_HEAD = GLA_DK // GLA_HEADS
DV_HEAD = GLA_DV // GLA_HEADS
GATE_RANK = 16
GATE_NORM = 16.0
GLA_MAIN = 2 * GLA_DK + 2 * GLA_DV
CONV_DIM = 2048
CONV_WIDTH = 31
CONV_PAD = CONV_WIDTH - 1
EPS = 1e-6

SUBLANES = 8
LANES = 128
HALO = 32
ROW_STRIDE = 2
TM = 256
TM_OUT = 512
W_CHUNK = 256
SEG = 64
VMEM_LIMIT = 56 * 1024 * 1024


def _cparams(sem):
    return pltpu.CompilerParams(dimension_semantics=sem, vmem_limit_bytes=VMEM_LIMIT)


def _after(x, dep):
    half = jnp.uint32(16)
    zero = lax.shift_right_logical(lax.shift_right_logical(lax.bitcast_convert_type(dep, jnp.uint32), half), half)
    return lax.bitcast_convert_type(lax.bitcast_convert_type(x, jnp.uint32) | zero, F32)


def _silu(x):
    return x * jax.nn.sigmoid(x)


def _rmsnorm_rows(x, g):
    ms = jnp.mean(x * x, axis=-1, keepdims=True)
    return x * lax.rsqrt(ms + EPS) * g


def _resident(shape):
    return pl.BlockSpec(shape, lambda *_: (0,) * len(shape), pipeline_mode=pl.Buffered(1))


def _tile_spec(width, lag, first, count, tm=TM):
    return pl.BlockSpec((tm, width), lambda s: (jnp.clip(s - lag - first, 0, count - 1), 0))


def _gla_front_kernel(xp_ref, xs_ref, g_ref, w_hbm, wa1_ref, wa2_ref, ba_ref, o_ref, la_ref, h_cur, h_nxt,
                      w_ref, stage, sem, *, n_p, tn):
    s = pl.program_id(0)

    def weight_chunk(j):
        return pltpu.make_async_copy(w_hbm.at[:, pl.ds(j * W_CHUNK, W_CHUNK)], stage.at[j % 2], sem.at[j % 2])

    @pl.when(s == 0)
    def _():
        h_nxt[...] = jnp.zeros_like(h_nxt)
        n_w = GLA_MAIN // W_CHUNK
        weight_chunk(0).start()
        for j in range(n_w):
            if j + 1 < n_w:
                weight_chunk(j + 1).start()
            weight_chunk(j).wait()
            w_ref[:, j * W_CHUNK:(j + 1) * W_CHUNK] = stage[j % 2].astype(BF16)

    h_cur[...] = h_nxt[...]

    h = h_cur[...]
    a1 = jnp.dot(h, wa1_ref[...], preferred_element_type=F32)
    z = jnp.dot(a1.astype(BF16), wa2_ref[...], preferred_element_type=F32) + ba_ref[...]
    la_ref[...] = (jnp.minimum(z, 0.0) - jnp.log(1.0 + jnp.exp(-jnp.abs(z)))) * (1.0 / GATE_NORM)
    for n0 in range(0, GLA_MAIN, tn):
        cols = slice(n0, n0 + tn)
        o_ref[:, cols] = jnp.dot(h_cur[...], w_ref[:, cols], preferred_element_type=F32).astype(BF16)

    x = jnp.where(s < n_p, xp_ref[...], xs_ref[...])
    h_nxt[...] = _rmsnorm_rows(x, g_ref[...]).astype(BF16)


def _gla_front(xp, xs, g, w_in_f32, wa1p, wa2p, ba, tn=512):
    n_p, n_s = xp.shape[0] // TM, xs.shape[0] // TM
    n = n_p + n_s
    return pl.pallas_call(
        functools.partial(_gla_front_kernel, n_p=n_p, tn=tn),
        grid=(n + 1,),
        in_specs=[
            _tile_spec(D_MODEL, 0, 0, n_p),
            _tile_spec(D_MODEL, 0, n_p, n_s),
            _resident((1, D_MODEL)),
            pl.BlockSpec(memory_space=pl.ANY),
            _resident((D_MODEL, LANES)),
            _resident((LANES, GLA_DK)),
            _resident((1, GLA_DK)),
        ],
        out_specs=[_tile_spec(GLA_MAIN, 1, 0, n), _tile_spec(GLA_DK, 1, 0, n)],
        out_shape=[
            jax.ShapeDtypeStruct((n * TM, GLA_MAIN), BF16),
            jax.ShapeDtypeStruct((n * TM, GLA_DK), F32),
        ],
        scratch_shapes=[
            pltpu.VMEM((TM, D_MODEL), BF16),
            pltpu.VMEM((TM, D_MODEL), BF16),
            pltpu.VMEM((D_MODEL, GLA_MAIN), BF16),
            pltpu.VMEM((2, D_MODEL, W_CHUNK), F32),
            pltpu.SemaphoreType.DMA((2,)),
        ],
        compiler_params=_cparams(("arbitrary",)),
        name="gla_front",
    )(xp, xs, g, w_in_f32, wa1p, wa2p, ba)


def _chunk_cumsum(la_ref, r0):
    sub = lax.broadcasted_iota(jnp.int32, (SUBLANES, GLA_DK), 0)
    groups = []
    carry = None
    for gi in range(CHUNK // SUBLANES):
        x = la_ref[pl.ds(r0 + gi * SUBLANES, SUBLANES), :]
        for s in (1, 2, 4):
            x = x + jnp.where(sub >= s, pltpu.roll(x, s, 0), 0.0)
        if carry is not None:
            x = x + carry
        carry = x[SUBLANES - 1:SUBLANES, :]
        groups.append(x)
    return jnp.concatenate(groups, axis=0)


def _gla_core_kernel(*refs, n_chunks, has_state):
    if has_state:
        q_ref, k_ref, v_ref, r_ref, la_ref, gn_ref, s0_ref, og_ref, so_ref, s_ref = refs
    else:
        q_ref, k_ref, v_ref, r_ref, la_ref, gn_ref, og_ref, so_ref, s_ref = refs
    t = pl.program_id(1)

    @pl.when(t == 0)
    def _():
        if has_state:
            s_ref[...] = s0_ref[0]
        else:
            s_ref[...] = jnp.zeros_like(s_ref)

    row = lax.broadcasted_iota(jnp.int32, (CHUNK, CHUNK), 0)
    col = lax.broadcasted_iota(jnp.int32, (CHUNK, CHUNK), 1)
    causal = row >= col

    def chunk_body(c, carry):
        r0 = pl.multiple_of(c * CHUNK, CHUNK)
        rows = pl.ds(r0, CHUNK)
        b = _chunk_cumsum(la_ref, r0)
        bl = b[CHUNK - 1:CHUNK, :]
        q = q_ref[rows, :].astype(F32)
        k = k_ref[rows, :].astype(F32)
        qe = (q * (jnp.exp(b) * (DK_HEAD ** -0.5))).astype(BF16)
        ke = (k * jnp.exp(-b)).astype(BF16)
        kd_t = jnp.transpose(k * jnp.exp(bl - b)).astype(BF16)
        dec_t = jnp.transpose(jnp.broadcast_to(jnp.exp(bl), (SUBLANES, GLA_DK)))
        for h in range(GLA_HEADS):
            ks = slice(h * DK_HEAD, (h + 1) * DK_HEAD)
            vs = slice(h * DV_HEAD, (h + 1) * DV_HEAD)
            qh = qe[:, ks]
            vh = v_ref[rows, vs]
            att = lax.dot_general(qh, ke[:, ks], (((1,), (1,)), ((), ())), preferred_element_type=F32)
            att = jnp.where(causal, att, 0.0).astype(BF16)
            s_old = s_ref[h]
            o = jnp.dot(att, vh, preferred_element_type=F32) + jnp.dot(
                qh, s_old.astype(BF16), preferred_element_type=F32)
            s_ref[h] = dec_t[ks, 0:1] * s_old + jnp.dot(kd_t[ks, :], vh, preferred_element_type=F32)
            on = _rmsnorm_rows(o, gn_ref[:, vs])
            gate = _silu(r_ref[rows, vs].astype(F32))
            og_ref[rows, vs] = (on * gate).astype(BF16)
        return carry

    lax.fori_loop(0, n_chunks, chunk_body, 0, unroll=2 if n_chunks % 2 == 0 else 1)

    @pl.when(t == pl.num_programs(1) - 1)
    def _():
        so_ref[0] = s_ref[...]


def _gla_core(qkvr, loga, gnorm, s0, n_seq, seq_len, row0, tt):
    has_state = s0 is not None
    tps = seq_len // tt
    blk0 = row0 // tt

    def rows(width, col_blk):
        return pl.BlockSpec((tt, width), lambda b, t: (blk0 + b * tps + t, col_blk))

    in_specs = [rows(GLA_DK, 0), rows(GLA_DK, 1), rows(GLA_DV, 1), rows(GLA_DV, 2), rows(GLA_DK, 0),
                pl.BlockSpec((1, GLA_DV), lambda b, t: (0, 0))]
    args = [qkvr, qkvr, qkvr, qkvr, loga, gnorm]
    state_spec = pl.BlockSpec((1, GLA_HEADS, DK_HEAD, DV_HEAD), lambda b, t: (b, 0, 0, 0))
    if has_state:
        in_specs.append(state_spec)
        args.append(s0)
    return pl.pallas_call(
        functools.partial(_gla_core_kernel, n_chunks=tt // CHUNK, has_state=has_state),
        grid=(n_seq, tps),
        in_specs=in_specs,
        out_specs=[pl.BlockSpec((tt, GLA_DV), lambda b, t: (b * tps + t, 0)), state_spec],
        out_shape=[
            jax.ShapeDtypeStruct((n_seq * seq_len, GLA_DV), BF16),
            jax.ShapeDtypeStruct((n_seq, GLA_HEADS, DK_HEAD, DV_HEAD), F32),
        ],
        scratch_shapes=[pltpu.VMEM((GLA_HEADS, DK_HEAD, DV_HEAD), F32)],
        compiler_params=_cparams(("arbitrary", "arbitrary")),
        name="gla_core",
    )(*args)


def _outproj0_kernel(ap_ref, as_ref, w_ref, xp_ref, xs_ref, o_ref, *, n_p, tn):
    is_p = pl.program_id(0) < n_p
    a = jnp.where(is_p, ap_ref[...], as_ref[...])
    for n0 in range(0, D_MODEL, tn):
        cols = slice(n0, n0 + tn)
        x = jnp.where(is_p, xp_ref[:, cols], xs_ref[:, cols])
        o_ref[:, cols] = x + jnp.dot(a, w_ref[:, cols], preferred_element_type=F32)


def _outproj0(og_p, og_s, w, xp, xs, tn=512):
    n_p, n_s = xp.shape[0] // TM, xs.shape[0] // TM
    n = n_p + n_s
    return pl.pallas_call(
        functools.partial(_outproj0_kernel, n_p=n_p, tn=tn),
        grid=(n,),
        in_specs=[
            _tile_spec(GLA_DV, 0, 0, n_p),
            _tile_spec(GLA_DV, 0, n_p, n_s),
            _resident((GLA_DV, D_MODEL)),
            _tile_spec(D_MODEL, 0, 0, n_p),
            _tile_spec(D_MODEL, 0, n_p, n_s),
        ],
        out_specs=_tile_spec(D_MODEL, 0, 0, n),
        out_shape=jax.ShapeDtypeStruct((n * TM, D_MODEL), F32),
        compiler_params=_cparams(("arbitrary",)),
        name="outproj0",
    )(og_p, og_s, w, xp, xs)


def _time_rows(start, n):
    return pl.ds(ROW_STRIDE * start, n, stride=ROW_STRIDE)


def _conv_front_kernel(x_ref, g_ref, w_ref, b_ref, cw_ref, cb_ref, st_ref, sz_ref, c_ref, sop_ref, sos_ref,
                       h_cur, h_nxt, u_stage, buf, hist, *, n_p, tiles_per_seq, tn):
    s = pl.program_id(0)
    nseg = TM // SEG
    region = HALO + SEG
    n_slab = CONV_DIM // LANES
    first = HALO - CONV_PAD

    @pl.when(s == 0)
    def _():
        h_nxt[...] = jnp.zeros_like(h_nxt)
        u_stage[...] = jnp.zeros_like(u_stage)
        hist[...] = jnp.zeros_like(hist)

    h_cur[...] = h_nxt[...]
    t_conv = jnp.maximum(s - 2, 0)
    is_p = t_conv < n_p
    for g in range(nseg):
        for sl in range(n_slab):
            lanes = slice(sl * LANES, (sl + 1) * LANES)
            buf[sl, _time_rows(g * region + HALO, SEG), :] = u_stage[g * SEG:(g + 1) * SEG, lanes]

    @pl.when(is_p)
    def _():
        @pl.when(lax.rem(t_conv, tiles_per_seq) == 0)
        def _():
            hist[...] = jnp.zeros_like(hist)

        for sl in range(n_slab):
            lanes = slice(sl * LANES, (sl + 1) * LANES)
            buf[sl, _time_rows(0, HALO), :] = hist[:, lanes]
            for g in range(1, nseg):
                buf[sl, _time_rows(g * region, HALO), :] = u_stage[g * SEG - HALO:g * SEG, lanes]
        tail = u_stage[TM - HALO:TM, :]
        hist[...] = tail

        @pl.when(lax.rem(t_conv, tiles_per_seq) == tiles_per_seq - 1)
        def _():
            sop_ref[t_conv // tiles_per_seq] = tail

    @pl.when(jnp.logical_not(is_p))
    def _():
        for g in range(nseg):
            seq_i = (t_conv - n_p) * nseg + g
            for sl in range(n_slab):
                lanes = slice(sl * LANES, (sl + 1) * LANES)
                buf[sl, _time_rows(g * region, HALO), :] = st_ref[seq_i, :, lanes]
            sos_ref[seq_i] = u_stage[(g + 1) * SEG - HALO:(g + 1) * SEG, :]

    n_chunks = CONV_DIM // tn
    slabs_per_chunk = n_slab // n_chunks
    prev_acc = None
    for j in range(n_chunks):
        n0 = j * tn
        cols = slice(n0, n0 + tn)
        gcols = slice(CONV_DIM + n0, CONV_DIM + n0 + tn)
        zcols = slice(2 * CONV_DIM + n0, 2 * CONV_DIM + n0 + tn)
        a = jnp.dot(h_cur[...], w_ref[:, cols], preferred_element_type=F32) + b_ref[:, cols]
        ga = jnp.dot(h_cur[...], w_ref[:, gcols], preferred_element_type=F32) + b_ref[:, gcols]
        u_stage[:, cols] = a * jax.nn.sigmoid(ga)
        z = jnp.dot(h_cur[...], w_ref[:, zcols], preferred_element_type=F32) + b_ref[:, zcols]
        sz_ref[:, cols] = _silu(z).astype(BF16)

        for sl in range(j * slabs_per_chunk, (j + 1) * slabs_per_chunk):
            lanes = slice(sl * LANES, (sl + 1) * LANES)
            for g in range(nseg):
                acc = jnp.broadcast_to(cb_ref[:, lanes], (SEG, LANES))
                if prev_acc is not None:
                    acc = _after(acc, prev_acc)
                for kk in range(CONV_WIDTH):
                    acc = acc + buf[sl, _time_rows(g * region + first + kk, SEG), :] * cw_ref[kk:kk + 1, lanes]
                c_ref[g * SEG:(g + 1) * SEG, lanes] = acc
                prev_acc = acc

    h_nxt[...] = _rmsnorm_rows(x_ref[...], g_ref[...]).astype(BF16)


def _conv_front(x1, g, w, b, cw, cb, st, n_p, n_prompt_seq, tn=512):
    n = x1.shape[0] // TM
    n_stream = st.shape[0]
    return pl.pallas_call(
        functools.partial(_conv_front_kernel, n_p=n_p, tiles_per_seq=n_p // n_prompt_seq, tn=tn),
        grid=(n + 2,),
        in_specs=[
            _tile_spec(D_MODEL, 0, 0, n),
            _resident((1, D_MODEL)),
            _resident((D_MODEL, 3 * CONV_DIM)),
            _resident((1, 3 * CONV_DIM)),
            _resident((CONV_WIDTH, CONV_DIM)),
            _resident((1, CONV_DIM)),
            _resident((n_stream, HALO, CONV_DIM)),
        ],
        out_specs=[
            _tile_spec(CONV_DIM, 1, 0, n),
            _tile_spec(CONV_DIM, 2, 0, n),
            pl.BlockSpec((n_prompt_seq, HALO, CONV_DIM), lambda s: (0, 0, 0)),
            pl.BlockSpec((n_stream, HALO, CONV_DIM), lambda s: (0, 0, 0)),
        ],
        out_shape=[
            jax.ShapeDtypeStruct((n * TM, CONV_DIM), BF16),
            jax.ShapeDtypeStruct((n * TM, CONV_DIM), F32),
            jax.ShapeDtypeStruct((n_prompt_seq, HALO, CONV_DIM), F32),
            jax.ShapeDtypeStruct((n_stream, HALO, CONV_DIM), F32),
        ],
        scratch_shapes=[
            pltpu.VMEM((TM, D_MODEL), BF16),
            pltpu.VMEM((TM, D_MODEL), BF16),
            pltpu.VMEM((TM, CONV_DIM), F32),
            pltpu.VMEM((CONV_DIM // LANES, ROW_STRIDE * (TM // SEG) * (HALO + SEG), LANES), F32),
            pltpu.VMEM((HALO, CONV_DIM), F32),
        ],
        compiler_params=_cparams(("arbitrary",)),
        name="conv_front",
    )(x1, g, w, b, cw, cb, st)


def _outproj1_kernel(c_ref, sz_ref, lg_ref, lb_ref, w_ref, x_ref, fg_ref, yp_ref, ys_ref, a_scr, y_scr,
                     *, n_p, tn):
    c = c_ref[...]
    mu = jnp.mean(c, axis=-1, keepdims=True)
    xc = c - mu
    ln = xc * lax.rsqrt(jnp.mean(xc * xc, axis=-1, keepdims=True) + EPS) * lg_ref[...] + lb_ref[...]
    a_scr[...] = (_silu(ln) * sz_ref[...].astype(F32)).astype(BF16)

    for n0 in range(0, D_MODEL, tn):
        cols = slice(n0, n0 + tn)
        y_scr[:, cols] = x_ref[:, cols] + jnp.dot(a_scr[...], w_ref[:, cols], preferred_element_type=F32)
    y = y_scr[...]
    scale = lax.rsqrt(jnp.mean(y * y, axis=-1, keepdims=True) + EPS)
    is_p = pl.program_id(0) < n_p

    @pl.when(is_p)
    def _():
        yp_ref[...] = y_scr[...] * scale * fg_ref[...]

    @pl.when(jnp.logical_not(is_p))
    def _():
        ys_ref[...] = y_scr[...] * scale * fg_ref[...]


def _outproj1(c, sz, lg, lb, w, x1, fg, n_p_rows, tn=512):
    n = x1.shape[0] // TM_OUT
    n_p = n_p_rows // TM_OUT
    n_s = n - n_p
    vec = _resident((1, D_MODEL))
    rows = functools.partial(_tile_spec, tm=TM_OUT)
    return pl.pallas_call(
        functools.partial(_outproj1_kernel, n_p=n_p, tn=tn),
        grid=(n,),
        in_specs=[
            rows(CONV_DIM, 0, 0, n),
            rows(CONV_DIM, 0, 0, n),
            vec, vec,
            _resident((CONV_DIM, D_MODEL)),
            rows(D_MODEL, 0, 0, n),
            vec,
        ],
        out_specs=[rows(D_MODEL, 0, 0, n_p), rows(D_MODEL, 0, n_p, n_s)],
        out_shape=[
            jax.ShapeDtypeStruct((n_p * TM_OUT, D_MODEL), F32),
            jax.ShapeDtypeStruct((n_s * TM_OUT, D_MODEL), F32),
        ],
        scratch_shapes=[pltpu.VMEM((TM_OUT, CONV_DIM), BF16), pltpu.VMEM((TM_OUT, D_MODEL), F32)],
        compiler_params=_cparams(("arbitrary",)),
        name="outproj1",
    )(c, sz, lg, lb, w, x1, fg)


def _forward(x_prompt, x_sample, gla_state, conv_hist, p, tt_prompt=512):
    n_pseq, p_len, _ = x_prompt.shape
    n_sseq, s_len, _ = x_sample.shape
    assert s_len == SEG and p_len % TM == 0 and (n_sseq * s_len) % TM == 0
    xp = x_prompt.reshape(n_pseq * p_len, D_MODEL)
    xs = x_sample.reshape(n_sseq * s_len, D_MODEL)
    n_p = xp.shape[0] // TM

    qkvr, loga = _gla_front(xp, xs, p["norm_g0"], p["gla_w_in"], p["gla_wa1"], p["gla_wa2"], p["gla_ba"])
    og_p, gla_p = _gla_core(qkvr, loga, p["gla_gn"], None, n_pseq, p_len, 0, tt_prompt)
    og_s, gla_s = _gla_core(qkvr, loga, p["gla_gn"], gla_state, n_sseq, s_len, xp.shape[0], s_len)
    x1 = _outproj0(og_p, og_s, p["gla_w_out"], xp, xs)

    sz, c, conv_p, conv_s = _conv_front(x1, p["norm_g1"], p["conv_w_in"], p["conv_b_in"], p["conv_w"],
                                        p["conv_b"], conv_hist, n_p, n_pseq)
    y_p, y_s = _outproj1(c, sz, p["conv_ln_g"], p["conv_ln_b"], p["conv_w_out"], x1, p["final_g"], xp.shape[0])
    keep = slice(HALO - CONV_PAD, HALO)
    return (y_p.reshape(x_prompt.shape), y_s.reshape(x_sample.shape), gla_p[None], conv_p[None, :, keep, :],
            gla_s[None], conv_s[None, :, keep, :])


def _prepare(norm_g, final_norm_g, gla_w_in, gla_w_a2, gla_b_a, gla_norm_g, gla_w_out, conv_w_in, conv_b_in,
             conv_w, conv_b, conv_ln_g, conv_ln_b, conv_w_out):
    w_in0 = gla_w_in[0]
    return {
        "norm_g0": norm_g[0:1],
        "norm_g1": norm_g[1:2],
        "final_g": final_norm_g[None, :],
        "gla_w_in": w_in0,
        "gla_wa1": jnp.pad(w_in0[:, GLA_MAIN:], ((0, 0), (0, LANES - GATE_RANK))).astype(BF16),
        "gla_wa2": jnp.pad(gla_w_a2[0], ((0, LANES - GATE_RANK), (0, 0))).astype(BF16),
        "gla_ba": gla_b_a[0:1],
        "gla_gn": gla_norm_g[0:1],
        "gla_w_out": gla_w_out[0].astype(BF16),
        "conv_w_in": conv_w_in[0].astype(BF16),
        "conv_b_in": conv_b_in[0:1],
        "conv_w": conv_w[0],
        "conv_b": conv_b[0:1],
        "conv_ln_g": conv_ln_g[0:1],
        "conv_ln_b": conv_ln_b[0:1],
        "conv_w_out": conv_w_out[0].astype(BF16),
    }


def kernel(x_prompt, x_sample, state_gla, state_conv, norm_g, final_norm_g, gla_w_in, gla_w_a2, gla_b_a,
           gla_norm_g, gla_w_out, conv_w_in, conv_b_in, conv_w, conv_b, conv_ln_g, conv_ln_b, conv_w_out):
    p = _prepare(norm_g, final_norm_g, gla_w_in, gla_w_a2, gla_b_a, gla_norm_g, gla_w_out, conv_w_in, conv_b_in,
                 conv_w, conv_b, conv_ln_g, conv_ln_b, conv_w_out)
    conv_hist = jnp.pad(state_conv[0], ((0, 0), (HALO - CONV_PAD, 0), (0, 0)))
    return _forward(x_prompt, x_sample, state_gla[0], conv_hist, p)
```

```python
import functools

import jax
import jax.numpy as jnp
from jax import lax
from jax.experimental import pallas as pl
from jax.experimental.pallas import tpu as pltpu

F32 = jnp.float32
BF16 = jnp.bfloat16

D_MODEL = 2048
CHUNK = 64
GLA_HEADS = 4
GLA_DK = 1024
GLA_DV = 2048
DK_HEAD = GLA_DK // GLA_HEADS
DV_HEAD = GLA_DV // GLA_HEADS
GATE_RANK = 16
GATE_NORM = 16.0
GLA_MAIN = 2 * GLA_DK + 2 * GLA_DV
CONV_DIM = 2048
CONV_WIDTH = 31
CONV_PAD = CONV_WIDTH - 1
EPS = 1e-6

SUBLANES = 8
LANES = 128
HALO = 32
ROW_STRIDE = 2
TM = 256
TM_OUT = 512
W_CHUNK = 256
SEG = 64
VMEM_LIMIT = 56 * 1024 * 1024


def _cparams(sem):
    return pltpu.CompilerParams(dimension_semantics=sem, vmem_limit_bytes=VMEM_LIMIT)


def _after(x, dep):
    half = jnp.uint32(16)
    zero = lax.shift_right_logical(lax.shift_right_logical(lax.bitcast_convert_type(dep, jnp.uint32), half), half)
    return lax.bitcast_convert_type(lax.bitcast_convert_type(x, jnp.uint32) | zero, F32)


def _silu(x):
    return x * jax.nn.sigmoid(x)


def _rmsnorm_rows(x, g):
    ms = jnp.mean(x * x, axis=-1, keepdims=True)
    return x * lax.rsqrt(ms + EPS) * g


def _resident(shape):
    return pl.BlockSpec(shape, lambda *_: (0,) * len(shape), pipeline_mode=pl.Buffered(1))


def _tile_spec(width, lag, first, count, tm=TM):
    return pl.BlockSpec((tm, width), lambda s: (jnp.clip(s - lag - first, 0, count - 1), 0))


def _gla_front_kernel(xp_ref, xs_ref, g_ref, w_hbm, wa2_ref, ba_ref, o_ref, la_ref, h_cur, h_nxt,
                      w_ref, wa1t_ref, stage, sem, *, n_p, tn):
    s = pl.program_id(0)

    def weight_chunk(j):
        return pltpu.make_async_copy(w_hbm.at[pl.ds(j * W_CHUNK, W_CHUNK), :], stage.at[j % 2], sem.at[j % 2])

    @pl.when(s == 0)
    def _():
        h_nxt[...] = jnp.zeros_like(h_nxt)
        n_w = GLA_MAIN // W_CHUNK
        weight_chunk(0).start()
        for j in range(n_w):
            if j + 1 < n_w:
                weight_chunk(j + 1).start()
            weight_chunk(j).wait()
            w_ref[:, j * W_CHUNK:(j + 1) * W_CHUNK] = jnp.transpose(stage[j % 2]).astype(BF16)
        gate_rows = pltpu.make_async_copy(w_hbm.at[pl.ds(GLA_MAIN, GATE_RANK), :],
                                          stage.at[0, pl.ds(0, GATE_RANK), :], sem.at[0])
        gate_rows.start()
        gate_rows.wait()
        wa1t_ref[...] = jnp.zeros_like(wa1t_ref)
        wa1t_ref[0:GATE_RANK, :] = stage[0, 0:GATE_RANK, :].astype(BF16)

    h_cur[...] = h_nxt[...]

    h = h_cur[...]
    a1 = lax.dot_general(h, wa1t_ref[...], (((1,), (1,)), ((), ())), preferred_element_type=F32)
    z = jnp.dot(a1.astype(BF16), wa2_ref[...], preferred_element_type=F32) + ba_ref[...]
    la_ref[...] = (jnp.minimum(z, 0.0) - jnp.log(1.0 + jnp.exp(-jnp.abs(z)))) * (1.0 / GATE_NORM)
    for n0 in range(0, GLA_MAIN, tn):
        cols = slice(n0, n0 + tn)
        o_ref[:, cols] = jnp.dot(h_cur[...], w_ref[:, cols], preferred_element_type=F32).astype(BF16)

    x = jnp.where(s < n_p, xp_ref[...], xs_ref[...])
    h_nxt[...] = _rmsnorm_rows(x, g_ref[...]).astype(BF16)


def _gla_front(xp, xs, g, w_in_f32, wa2p, ba, tn=512):
    n_p, n_s = xp.shape[0] // TM, xs.shape[0] // TM
    n = n_p + n_s
    return pl.pallas_call(
        functools.partial(_gla_front_kernel, n_p=n_p, tn=tn),
        grid=(n + 1,),
        in_specs=[
            _tile_spec(D_MODEL, 0, 0, n_p),
            _tile_spec(D_MODEL, 0, n_p, n_s),
            _resident((1, D_MODEL)),
            pl.BlockSpec(memory_space=pl.ANY),
            _resident((LANES, GLA_DK)),
            _resident((1, GLA_DK)),
        ],
        out_specs=[_tile_spec(GLA_MAIN, 1, 0, n), _tile_spec(GLA_DK, 1, 0, n)],
        out_shape=[
            jax.ShapeDtypeStruct((n * TM, GLA_MAIN), BF16),
            jax.ShapeDtypeStruct((n * TM, GLA_DK), F32),
        ],
        scratch_shapes=[
            pltpu.VMEM((TM, D_MODEL), BF16),
            pltpu.VMEM((TM, D_MODEL), BF16),
            pltpu.VMEM((D_MODEL, GLA_MAIN), BF16),
            pltpu.VMEM((LANES, D_MODEL), BF16),
            pltpu.VMEM((2, W_CHUNK, D_MODEL), F32),
            pltpu.SemaphoreType.DMA((2,)),
        ],
        compiler_params=_cparams(("arbitrary",)),
        name="gla_front",
    )(xp, xs, g, w_in_f32, wa2p, ba)


def _chunk_cumsum(la_ref, r0):
    sub = lax.broadcasted_iota(jnp.int32, (SUBLANES, GLA_DK), 0)
    groups = []
    carry = None
    for gi in range(CHUNK // SUBLANES):
        x = la_ref[pl.ds(r0 + gi * SUBLANES, SUBLANES), :]
        for s in (1, 2, 4):
            x = x + jnp.where(sub >= s, pltpu.roll(x, s, 0), 0.0)
        if carry is not None:
            x = x + carry
        carry = x[SUBLANES - 1:SUBLANES, :]
        groups.append(x)
    return jnp.concatenate(groups, axis=0)


def _gla_core_kernel(*refs, n_chunks, has_state):
    if has_state:
        q_ref, k_ref, v_ref, r_ref, la_ref, gn_ref, s0_ref, og_ref, so_ref, s_ref = refs
    else:
        q_ref, k_ref, v_ref, r_ref, la_ref, gn_ref, og_ref, so_ref, s_ref = refs
    t = pl.program_id(1)

    @pl.when(t == 0)
    def _():
        if has_state:
            s_ref[...] = s0_ref[0]
        else:
            s_ref[...] = jnp.zeros_like(s_ref)

    row = lax.broadcasted_iota(jnp.int32, (CHUNK, CHUNK), 0)
    col = lax.broadcasted_iota(jnp.int32, (CHUNK, CHUNK), 1)
    causal = row >= col

    def chunk_body(c, carry):
        r0 = pl.multiple_of(c * CHUNK, CHUNK)
        rows = pl.ds(r0, CHUNK)
        b = _chunk_cumsum(la_ref, r0)
        bl = b[CHUNK - 1:CHUNK, :]
        q = q_ref[rows, :].astype(F32)
        k = k_ref[rows, :].astype(F32)
        qe = (q * (jnp.exp(b) * (DK_HEAD ** -0.5))).astype(BF16)
        ke = (k * jnp.exp(-b)).astype(BF16)
        kd_t = jnp.transpose(k * jnp.exp(bl - b)).astype(BF16)
        dec_t = jnp.transpose(jnp.broadcast_to(jnp.exp(bl), (SUBLANES, GLA_DK)))
        for h in range(GLA_HEADS):
            ks = slice(h * DK_HEAD, (h + 1) * DK_HEAD)
            vs = slice(h * DV_HEAD, (h + 1) * DV_HEAD)
            qh = qe[:, ks]
            vh = v_ref[rows, vs]
            att = lax.dot_general(qh, ke[:, ks], (((1,), (1,)), ((), ())), preferred_element_type=F32)
            att = jnp.where(causal, att, 0.0).astype(BF16)
            s_old = s_ref[h]
            o = jnp.dot(att, vh, preferred_element_type=F32) + jnp.dot(
                qh, s_old.astype(BF16), preferred_element_type=F32)
            s_ref[h] = dec_t[ks, 0:1] * s_old + jnp.dot(kd_t[ks, :], vh, preferred_element_type=F32)
            on = _rmsnorm_rows(o, gn_ref[:, vs])
            gate = _silu(r_ref[rows, vs].astype(F32))
            og_ref[rows, vs] = (on * gate).astype(BF16)
        return carry

    lax.fori_loop(0, n_chunks, chunk_body, 0, unroll=2 if n_chunks % 2 == 0 else 1)

    @pl.when(t == pl.num_programs(1) - 1)
    def _():
        so_ref[0] = s_ref[...]


def _gla_core(qkvr, loga, gnorm, s0, n_seq, seq_len, row0, tt):
    has_state = s0 is not None
    tps = seq_len // tt
    blk0 = row0 // tt

    def rows(width, col_blk):
        return pl.BlockSpec((tt, width), lambda b, t: (blk0 + b * tps + t, col_blk))

    in_specs = [rows(GLA_DK, 0), rows(GLA_DK, 1), rows(GLA_DV, 1), rows(GLA_DV, 2), rows(GLA_DK, 0),
                pl.BlockSpec((1, GLA_DV), lambda b, t: (0, 0))]
    args = [qkvr, qkvr, qkvr, qkvr, loga, gnorm]
    state_spec = pl.BlockSpec((1, GLA_HEADS, DK_HEAD, DV_HEAD), lambda b, t: (b, 0, 0, 0))
    if has_state:
        in_specs.append(state_spec)
        args.append(s0)
    return pl.pallas_call(
        functools.partial(_gla_core_kernel, n_chunks=tt // CHUNK, has_state=has_state),
        grid=(n_seq, tps),
        in_specs=in_specs,
        out_specs=[pl.BlockSpec((tt, GLA_DV), lambda b, t: (b * tps + t, 0)), state_spec],
        out_shape=[
            jax.ShapeDtypeStruct((n_seq * seq_len, GLA_DV), BF16),
            jax.ShapeDtypeStruct((n_seq, GLA_HEADS, DK_HEAD, DV_HEAD), F32),
        ],
        scratch_shapes=[pltpu.VMEM((GLA_HEADS, DK_HEAD, DV_HEAD), F32)],
        compiler_params=_cparams(("arbitrary", "arbitrary")),
        name="gla_core",
    )(*args)


def _outproj0_kernel(ap_ref, as_ref, w_ref, xp_ref, xs_ref, o_ref, *, n_p, tn):
    is_p = pl.program_id(0) < n_p
    a = jnp.where(is_p, ap_ref[...], as_ref[...])
    for n0 in range(0, D_MODEL, tn):
        cols = slice(n0, n0 + tn)
        x = jnp.where(is_p, xp_ref[:, cols], xs_ref[:, cols])
        o_ref[:, cols] = x + jnp.dot(a, w_ref[:, cols], preferred_element_type=F32)


def _outproj0(og_p, og_s, w, xp, xs, tn=512):
    n_p, n_s = xp.shape[0] // TM_OUT, xs.shape[0] // TM_OUT
    n = n_p + n_s
    rows = functools.partial(_tile_spec, tm=TM_OUT)
    return pl.pallas_call(
        functools.partial(_outproj0_kernel, n_p=n_p, tn=tn),
        grid=(n,),
        in_specs=[
            rows(GLA_DV, 0, 0, n_p),
            rows(GLA_DV, 0, n_p, n_s),
            _resident((GLA_DV, D_MODEL)),
            rows(D_MODEL, 0, 0, n_p),
            rows(D_MODEL, 0, n_p, n_s),
        ],
        out_specs=rows(D_MODEL, 0, 0, n),
        out_shape=jax.ShapeDtypeStruct((n * TM_OUT, D_MODEL), F32),
        compiler_params=_cparams(("arbitrary",)),
        name="outproj0",
    )(og_p, og_s, w, xp, xs)


def _time_rows(start, n):
    return pl.ds(ROW_STRIDE * start, n, stride=ROW_STRIDE)


def _conv_front_kernel(x_ref, g_ref, w_ref, b_ref, cw_ref, cb_ref, st_ref, sz_ref, c_ref, sop_ref, sos_ref,
                       h_cur, h_nxt, u_stage, buf, hist, *, n_p, tiles_per_seq, tn):
    s = pl.program_id(0)
    nseg = TM // SEG
    region = HALO + SEG
    n_slab = CONV_DIM // LANES
    first = HALO - CONV_PAD

    @pl.when(s == 0)
    def _():
        h_nxt[...] = jnp.zeros_like(h_nxt)
        u_stage[...] = jnp.zeros_like(u_stage)
        hist[...] = jnp.zeros_like(hist)

    h_cur[...] = h_nxt[...]
    t_conv = jnp.maximum(s - 2, 0)
    is_p = t_conv < n_p
    for g in range(nseg):
        for sl in range(n_slab):
            lanes = slice(sl * LANES, (sl + 1) * LANES)
            buf[sl, _time_rows(g * region + HALO, SEG), :] = u_stage[g * SEG:(g + 1) * SEG, lanes]

    @pl.when(is_p)
    def _():
        @pl.when(lax.rem(t_conv, tiles_per_seq) == 0)
        def _():
            hist[...] = jnp.zeros_like(hist)

        for sl in range(n_slab):
            lanes = slice(sl * LANES, (sl + 1) * LANES)
            buf[sl, _time_rows(0, HALO), :] = hist[:, lanes]
            for g in range(1, nseg):
                buf[sl, _time_rows(g * region, HALO), :] = u_stage[g * SEG - HALO:g * SEG, lanes]
        tail = u_stage[TM - HALO:TM, :]
        hist[...] = tail

        @pl.when(lax.rem(t_conv, tiles_per_seq) == tiles_per_seq - 1)
        def _():
            sop_ref[t_conv // tiles_per_seq] = tail

    @pl.when(jnp.logical_not(is_p))
    def _():
        for g in range(nseg):
            seq_i = (t_conv - n_p) * nseg + g
            for sl in range(n_slab):
                lanes = slice(sl * LANES, (sl + 1) * LANES)
                buf[sl, _time_rows(g * region, HALO), :] = st_ref[seq_i, :, lanes]
            sos_ref[seq_i] = u_stage[(g + 1) * SEG - HALO:(g + 1) * SEG, :]

    n_chunks = CONV_DIM // tn
    slabs_per_chunk = n_slab // n_chunks
    prev_acc = None
    for j in range(n_chunks):
        n0 = j * tn
        cols = slice(n0, n0 + tn)
        gcols = slice(CONV_DIM + n0, CONV_DIM + n0 + tn)
        zcols = slice(2 * CONV_DIM + n0, 2 * CONV_DIM + n0 + tn)
        a = jnp.dot(h_cur[...], w_ref[:, cols], preferred_element_type=F32) + b_ref[:, cols]
        ga = jnp.dot(h_cur[...], w_ref[:, gcols], preferred_element_type=F32) + b_ref[:, gcols]
        u_stage[:, cols] = a * jax.nn.sigmoid(ga)
        z = jnp.dot(h_cur[...], w_ref[:, zcols], preferred_element_type=F32) + b_ref[:, zcols]
        sz_ref[:, cols] = _silu(z).astype(BF16)

        for sl in range(j * slabs_per_chunk, (j + 1) * slabs_per_chunk):
            lanes = slice(sl * LANES, (sl + 1) * LANES)
            for g in range(nseg):
                acc = jnp.broadcast_to(cb_ref[:, lanes], (SEG, LANES))
                if prev_acc is not None:
                    acc = _after(acc, prev_acc)
                for kk in range(CONV_WIDTH):
                    acc = acc + buf[sl, _time_rows(g * region + first + kk, SEG), :] * cw_ref[kk:kk + 1, lanes]
                c_ref[g * SEG:(g + 1) * SEG, lanes] = acc
                prev_acc = acc

    h_nxt[...] = _rmsnorm_rows(x_ref[...], g_ref[...]).astype(BF16)


def _conv_front(x1, g, w, b, cw, cb, st, n_p, n_prompt_seq, tn=512):
    n = x1.shape[0] // TM
    n_stream = st.shape[0]
    return pl.pallas_call(
        functools.partial(_conv_front_kernel, n_p=n_p, tiles_per_seq=n_p // n_prompt_seq, tn=tn),
        grid=(n + 2,),
        in_specs=[
            _tile_spec(D_MODEL, 0, 0, n),
            _resident((1, D_MODEL)),
            _resident((D_MODEL, 3 * CONV_DIM)),
            _resident((1, 3 * CONV_DIM)),
            _resident((CONV_WIDTH, CONV_DIM)),
            _resident((1, CONV_DIM)),
            _resident((n_stream, HALO, CONV_DIM)),
        ],
        out_specs=[
            _tile_spec(CONV_DIM, 1, 0, n),
            _tile_spec(CONV_DIM, 2, 0, n),
            pl.BlockSpec((n_prompt_seq, HALO, CONV_DIM), lambda s: (0, 0, 0)),
            pl.BlockSpec((n_stream, HALO, CONV_DIM), lambda s: (0, 0, 0)),
        ],
        out_shape=[
            jax.ShapeDtypeStruct((n * TM, CONV_DIM), BF16),
            jax.ShapeDtypeStruct((n * TM, CONV_DIM), F32),
            jax.ShapeDtypeStruct((n_prompt_seq, HALO, CONV_DIM), F32),
            jax.ShapeDtypeStruct((n_stream, HALO, CONV_DIM), F32),
        ],
        scratch_shapes=[
            pltpu.VMEM((TM, D_MODEL), BF16),
            pltpu.VMEM((TM, D_MODEL), BF16),
            pltpu.VMEM((TM, CONV_DIM), F32),
            pltpu.VMEM((CONV_DIM // LANES, ROW_STRIDE * (TM // SEG) * (HALO + SEG), LANES), F32),
            pltpu.VMEM((HALO, CONV_DIM), F32),
        ],
        compiler_params=_cparams(("arbitrary",)),
        name="conv_front",
    )(x1, g, w, b, cw, cb, st)


def _outproj1_kernel(c_ref, sz_ref, lg_ref, lb_ref, w_ref, x_ref, fg_ref, yp_ref, ys_ref, a_scr, y_scr,
                     *, n_p, tn):
    c = c_ref[...]
    mu = jnp.mean(c, axis=-1, keepdims=True)
    xc = c - mu
    ln = xc * lax.rsqrt(jnp.mean(xc * xc, axis=-1, keepdims=True) + EPS) * lg_ref[...] + lb_ref[...]
    a_scr[...] = (_silu(ln) * sz_ref[...].astype(F32)).astype(BF16)

    for n0 in range(0, D_MODEL, tn):
        cols = slice(n0, n0 + tn)
        y_scr[:, cols] = x_ref[:, cols] + jnp.dot(a_scr[...], w_ref[:, cols], preferred_element_type=F32)
    y = y_scr[...]
    scale = lax.rsqrt(jnp.mean(y * y, axis=-1, keepdims=True) + EPS)
    is_p = pl.program_id(0) < n_p

    @pl.when(is_p)
    def _():
        yp_ref[...] = y_scr[...] * scale * fg_ref[...]

    @pl.when(jnp.logical_not(is_p))
    def _():
        ys_ref[...] = y_scr[...] * scale * fg_ref[...]


def _outproj1(c, sz, lg, lb, w, x1, fg, n_p_rows, tn=512):
    n = x1.shape[0] // TM_OUT
    n_p = n_p_rows // TM_OUT
    n_s = n - n_p
    vec = _resident((1, D_MODEL))
    rows = functools.partial(_tile_spec, tm=TM_OUT)
    return pl.pallas_call(
        functools.partial(_outproj1_kernel, n_p=n_p, tn=tn),
        grid=(n,),
        in_specs=[
            rows(CONV_DIM, 0, 0, n),
            rows(CONV_DIM, 0, 0, n),
            vec, vec,
            _resident((CONV_DIM, D_MODEL)),
            rows(D_MODEL, 0, 0, n),
            vec,
        ],
        out_specs=[rows(D_MODEL, 0, 0, n_p), rows(D_MODEL, 0, n_p, n_s)],
        out_shape=[
            jax.ShapeDtypeStruct((n_p * TM_OUT, D_MODEL), F32),
            jax.ShapeDtypeStruct((n_s * TM_OUT, D_MODEL), F32),
        ],
        scratch_shapes=[pltpu.VMEM((TM_OUT, CONV_DIM), BF16), pltpu.VMEM((TM_OUT, D_MODEL), F32)],
        compiler_params=_cparams(("arbitrary",)),
        name="outproj1",
    )(c, sz, lg, lb, w, x1, fg)


def _forward(x_prompt, x_sample, gla_state, conv_hist, p, tt_prompt=512):
    n_pseq, p_len, _ = x_prompt.shape
    n_sseq, s_len, _ = x_sample.shape
    assert s_len == SEG and p_len % TM == 0 and (n_sseq * s_len) % TM == 0
    xp = x_prompt.reshape(n_pseq * p_len, D_MODEL)
    xs = x_sample.reshape(n_sseq * s_len, D_MODEL)
    n_p = xp.shape[0] // TM

    qkvr, loga = _gla_front(xp, xs, p["norm_g0"], p["gla_w_in_t"], p["gla_wa2"], p["gla_ba"])
    og_p, gla_p = _gla_core(qkvr, loga, p["gla_gn"], None, n_pseq, p_len, 0, tt_prompt)
    og_s, gla_s = _gla_core(qkvr, loga, p["gla_gn"], gla_state, n_sseq, s_len, xp.shape[0], s_len)
    x1 = _outproj0(og_p, og_s, p["gla_w_out"], xp, xs)

    sz, c, conv_p, conv_s = _conv_front(x1, p["norm_g1"], p["conv_w_in"], p["conv_b_in"], p["conv_w"],
                                        p["conv_b"], conv_hist, n_p, n_pseq)
    y_p, y_s = _outproj1(c, sz, p["conv_ln_g"], p["conv_ln_b"], p["conv_w_out"], x1, p["final_g"], xp.shape[0])
    keep = slice(HALO - CONV_PAD, HALO)
    return (y_p.reshape(x_prompt.shape), y_s.reshape(x_sample.shape), gla_p[None], conv_p[None, :, keep, :],
            gla_s[None], conv_s[None, :, keep, :])


def _prepare(norm_g, final_norm_g, gla_w_in, gla_w_a2, gla_b_a, gla_norm_g, gla_w_out, conv_w_in, conv_b_in,
             conv_w, conv_b, conv_ln_g, conv_ln_b, conv_w_out):
    w_in0 = gla_w_in[0]
    return {
        "norm_g0": norm_g[0:1],
        "norm_g1": norm_g[1:2],
        "final_g": final_norm_g[None, :],
        "gla_w_in_t": jnp.swapaxes(w_in0, 0, 1),
        "gla_wa2": jnp.pad(gla_w_a2[0], ((0, LANES - GATE_RANK), (0, 0))).astype(BF16),
        "gla_ba": gla_b_a[0:1],
        "gla_gn": gla_norm_g[0:1],
        "gla_w_out": gla_w_out[0].astype(BF16),
        "conv_w_in": conv_w_in[0].astype(BF16),
        "conv_b_in": conv_b_in[0:1],
        "conv_w": conv_w[0],
        "conv_b": conv_b[0:1],
        "conv_ln_g": conv_ln_g[0:1],
        "conv_ln_b": conv_ln_b[0:1],
        "conv_w_out": conv_w_out[0].astype(BF16),
    }


def kernel(x_prompt, x_sample, state_gla, state_conv, norm_g, final_norm_g, gla_w_in, gla_w_a2, gla_b_a,
           gla_norm_g, gla_w_out, conv_w_in, conv_b_in, conv_w, conv_b, conv_ln_g, conv_ln_b, conv_w_out):
    p = _prepare(norm_g, final_norm_g, gla_w_in, gla_w_a2, gla_b_a, gla_norm_g, gla_w_out, conv_w_in, conv_b_in,
                 conv_w, conv_b, conv_ln_g, conv_ln_b, conv_w_out)
    conv_hist = jnp.pad(state_conv[0], ((0, 0), (HALO - CONV_PAD, 0), (0, 0)))
    return _forward(x_prompt, x_sample, state_gla[0], conv_hist, p)
```

```python
import functools

import jax
import jax.numpy as jnp
from jax import lax
from jax.experimental import pallas as pl
from jax.experimental.pallas import tpu as pltpu

F32 = jnp.float32
BF16 = jnp.bfloat16

D_MODEL = 2048
CHUNK = 64
GLA_HEADS = 4
GLA_DK = 1024
GLA_DV = 2048
DK_HEAD = GLA_DK // GLA_HEADS
DV_HEAD = GLA_DV // GLA_HEADS
GATE_RANK = 16
GATE_NORM = 16.0
GLA_MAIN = 2 * GLA_DK + 2 * GLA_DV
CONV_DIM = 2048
CONV_WIDTH = 31
CONV_PAD = CONV_WIDTH - 1
EPS = 1e-6
LOG2E = 1.4426950408889634

SUBLANES = 8
LANES = 128
HALO = 32
ROW_STRIDE = 2
TM = 256
TM_OUT = 512
W_CHUNK = 256
W_SLOTS = 4
SEG = 64
VMEM_LIMIT = 56 * 1024 * 1024


def _cparams(sem):
    return pltpu.CompilerParams(dimension_semantics=sem, vmem_limit_bytes=VMEM_LIMIT)


def _after(x, dep):
    half = jnp.uint32(16)
    zero = lax.shift_right_logical(lax.shift_right_logical(lax.bitcast_convert_type(dep, jnp.uint32), half), half)
    return lax.bitcast_convert_type(lax.bitcast_convert_type(x, jnp.uint32) | zero, F32)


def _silu(x):
    return x * jax.nn.sigmoid(x)


def _rmsnorm_rows(x, g):
    ms = jnp.mean(x * x, axis=-1, keepdims=True)
    return x * lax.rsqrt(ms + EPS) * g


def _resident(shape):
    return pl.BlockSpec(shape, lambda *_: (0,) * len(shape), pipeline_mode=pl.Buffered(1))


def _tile_spec(width, lag, first, count, tm=TM):
    return pl.BlockSpec((tm, width), lambda s: (jnp.clip(s - lag - first, 0, count - 1), 0))


def _gla_front_kernel(xp_ref, xs_ref, g_ref, w_hbm, wa2_ref, ba_ref, o_ref, la_ref, h_cur, h_nxt,
                      w_ref, wa1t_ref, stage, sem, *, n_p, tn):
    s = pl.program_id(0)

    def weight_chunk(j):
        slot = j % W_SLOTS
        return pltpu.make_async_copy(w_hbm.at[pl.ds(j * W_CHUNK, W_CHUNK), :], stage.at[slot], sem.at[slot])

    @pl.when(s == 0)
    def _():
        h_nxt[...] = jnp.zeros_like(h_nxt)
        n_w = GLA_MAIN // W_CHUNK
        for j in range(W_SLOTS - 1):
            weight_chunk(j).start()
        for j in range(n_w):
            if j + W_SLOTS - 1 < n_w:
                weight_chunk(j + W_SLOTS - 1).start()
            weight_chunk(j).wait()
            w_ref[:, j * W_CHUNK:(j + 1) * W_CHUNK] = jnp.transpose(stage[j % W_SLOTS]).astype(BF16)
        gate_rows = pltpu.make_async_copy(w_hbm.at[pl.ds(GLA_MAIN, GATE_RANK), :],
                                          stage.at[0, pl.ds(0, GATE_RANK), :], sem.at[0])
        gate_rows.start()
        gate_rows.wait()
        wa1t_ref[...] = jnp.zeros_like(wa1t_ref)
        wa1t_ref[0:GATE_RANK, :] = stage[0, 0:GATE_RANK, :].astype(BF16)

    h_cur[...] = h_nxt[...]

    h = h_cur[...]
    a1 = lax.dot_general(h, wa1t_ref[...], (((1,), (1,)), ((), ())), preferred_element_type=F32)
    z = jnp.dot(a1.astype(BF16), wa2_ref[...], preferred_element_type=F32) + ba_ref[...]
    la_ref[...] = (jnp.minimum(z, 0.0) - jnp.log(1.0 + jnp.exp(-jnp.abs(z)))) * (1.0 / GATE_NORM)
    for n0 in range(0, GLA_MAIN, tn):
        cols = slice(n0, n0 + tn)
        o_ref[:, cols] = jnp.dot(h_cur[...], w_ref[:, cols], preferred_element_type=F32).astype(BF16)

    x = jnp.where(s < n_p, xp_ref[...], xs_ref[...])
    h_nxt[...] = _rmsnorm_rows(x, g_ref[...]).astype(BF16)


def _gla_front(xp, xs, g, w_in_f32, wa2p, ba, tn=512):
    n_p, n_s = xp.shape[0] // TM, xs.shape[0] // TM
    n = n_p + n_s
    return pl.pallas_call(
        functools.partial(_gla_front_kernel, n_p=n_p, tn=tn),
        grid=(n + 1,),
        in_specs=[
            _tile_spec(D_MODEL, 0, 0, n_p),
            _tile_spec(D_MODEL, 0, n_p, n_s),
            _resident((1, D_MODEL)),
            pl.BlockSpec(memory_space=pl.ANY),
            _resident((LANES, GLA_DK)),
            _resident((1, GLA_DK)),
        ],
        out_specs=[_tile_spec(GLA_MAIN, 1, 0, n), _tile_spec(GLA_DK, 1, 0, n)],
        out_shape=[
            jax.ShapeDtypeStruct((n * TM, GLA_MAIN), BF16),
            jax.ShapeDtypeStruct((n * TM, GLA_DK), F32),
        ],
        scratch_shapes=[
            pltpu.VMEM((TM, D_MODEL), BF16),
            pltpu.VMEM((TM, D_MODEL), BF16),
            pltpu.VMEM((D_MODEL, GLA_MAIN), BF16),
            pltpu.VMEM((LANES, D_MODEL), BF16),
            pltpu.VMEM((W_SLOTS, W_CHUNK, D_MODEL), F32),
            pltpu.SemaphoreType.DMA((W_SLOTS,)),
        ],
        compiler_params=_cparams(("arbitrary",)),
        name="gla_front",
    )(xp, xs, g, w_in_f32, wa2p, ba)


def _chunk_cumsum(la_ref, r0):
    sub = lax.broadcasted_iota(jnp.int32, (SUBLANES, GLA_DK), 0)
    groups = []
    carry = None
    for gi in range(CHUNK // SUBLANES):
        x = la_ref[pl.ds(r0 + gi * SUBLANES, SUBLANES), :]
        for s in (1, 2, 4):
            x = x + jnp.where(sub >= s, pltpu.roll(x, s, 0), 0.0)
        if carry is not None:
            x = x + carry
        carry = x[SUBLANES - 1:SUBLANES, :]
        groups.append(x)
    return jnp.concatenate(groups, axis=0)


def _gla_core_kernel(*refs, n_chunks, has_state):
    if has_state:
        q_ref, k_ref, v_ref, r_ref, la_ref, gn_ref, s0_ref, og_ref, so_ref, s_ref = refs
    else:
        q_ref, k_ref, v_ref, r_ref, la_ref, gn_ref, og_ref, so_ref, s_ref = refs
    t = pl.program_id(1)

    @pl.when(t == 0)
    def _():
        if has_state:
            s_ref[...] = s0_ref[0]
        else:
            s_ref[...] = jnp.zeros_like(s_ref)

    row = lax.broadcasted_iota(jnp.int32, (CHUNK, CHUNK), 0)
    col = lax.broadcasted_iota(jnp.int32, (CHUNK, CHUNK), 1)
    causal = row >= col

    def chunk_body(c, carry):
        r0 = pl.multiple_of(c * CHUNK, CHUNK)
        rows = pl.ds(r0, CHUNK)
        b2 = _chunk_cumsum(la_ref, r0) * LOG2E
        bl2 = b2[CHUNK - 1:CHUNK, :]
        q = q_ref[rows, :].astype(F32)
        k = k_ref[rows, :].astype(F32)
        qe = (q * (jnp.exp2(b2) * (DK_HEAD ** -0.5))).astype(BF16)
        ke = (k * jnp.exp2(-b2)).astype(BF16)
        kd_t = jnp.transpose(k * jnp.exp2(bl2 - b2)).astype(BF16)
        dec_t = jnp.transpose(jnp.broadcast_to(jnp.exp2(bl2), (SUBLANES, GLA_DK)))
        atts = []
        for h in range(GLA_HEADS):
            ks = slice(h * DK_HEAD, (h + 1) * DK_HEAD)
            att = lax.dot_general(qe[:, ks], ke[:, ks], (((1,), (1,)), ((), ())), preferred_element_type=F32)
            atts.append(jnp.where(causal, att, 0.0).astype(BF16))
        for h in range(GLA_HEADS):
            ks = slice(h * DK_HEAD, (h + 1) * DK_HEAD)
            vs = slice(h * DV_HEAD, (h + 1) * DV_HEAD)
            qh = qe[:, ks]
            vh = v_ref[rows, vs]
            s_old = s_ref[h]
            o = jnp.dot(qh, s_old.astype(BF16), preferred_element_type=F32) + jnp.dot(
                atts[h], vh, preferred_element_type=F32)
            s_ref[h] = dec_t[ks, 0:1] * s_old + jnp.dot(kd_t[ks, :], vh, preferred_element_type=F32)
            on = _rmsnorm_rows(o, gn_ref[:, vs])
            gate = _silu(r_ref[rows, vs].astype(F32))
            og_ref[rows, vs] = (on * gate).astype(BF16)
        return carry

    lax.fori_loop(0, n_chunks, chunk_body, 0, unroll=2 if n_chunks % 2 == 0 else 1)

    @pl.when(t == pl.num_programs(1) - 1)
    def _():
        so_ref[0] = s_ref[...]


def _gla_core(qkvr, loga, gnorm, s0, n_seq, seq_len, row0, tt):
    has_state = s0 is not None
    tps = seq_len // tt
    blk0 = row0 // tt

    def rows(width, col_blk):
        return pl.BlockSpec((tt, width), lambda b, t: (blk0 + b * tps + t, col_blk))

    in_specs = [rows(GLA_DK, 0), rows(GLA_DK, 1), rows(GLA_DV, 1), rows(GLA_DV, 2), rows(GLA_DK, 0),
                pl.BlockSpec((1, GLA_DV), lambda b, t: (0, 0))]
    args = [qkvr, qkvr, qkvr, qkvr, loga, gnorm]
    state_spec = pl.BlockSpec((1, GLA_HEADS, DK_HEAD, DV_HEAD), lambda b, t: (b, 0, 0, 0))
    if has_state:
        in_specs.append(state_spec)
        args.append(s0)
    return pl.pallas_call(
        functools.partial(_gla_core_kernel, n_chunks=tt // CHUNK, has_state=has_state),
        grid=(n_seq, tps),
        in_specs=in_specs,
        out_specs=[pl.BlockSpec((tt, GLA_DV), lambda b, t: (b * tps + t, 0)), state_spec],
        out_shape=[
            jax.ShapeDtypeStruct((n_seq * seq_len, GLA_DV), BF16),
            jax.ShapeDtypeStruct((n_seq, GLA_HEADS, DK_HEAD, DV_HEAD), F32),
        ],
        scratch_shapes=[pltpu.VMEM((GLA_HEADS, DK_HEAD, DV_HEAD), F32)],
        compiler_params=_cparams(("arbitrary", "arbitrary")),
        name="gla_core",
    )(*args)


def _outproj0_kernel(ap_ref, as_ref, w_ref, xp_ref, xs_ref, o_ref, *, n_p, tn):
    is_p = pl.program_id(0) < n_p
    a = jnp.where(is_p, ap_ref[...], as_ref[...])
    for n0 in range(0, D_MODEL, tn):
        cols = slice(n0, n0 + tn)
        x = jnp.where(is_p, xp_ref[:, cols], xs_ref[:, cols])
        o_ref[:, cols] = x + jnp.dot(a, w_ref[:, cols], preferred_element_type=F32)


def _outproj0(og_p, og_s, w, xp, xs, tn=512):
    n_p, n_s = xp.shape[0] // TM_OUT, xs.shape[0] // TM_OUT
    n = n_p + n_s
    rows = functools.partial(_tile_spec, tm=TM_OUT)
    return pl.pallas_call(
        functools.partial(_outproj0_kernel, n_p=n_p, tn=tn),
        grid=(n,),
        in_specs=[
            rows(GLA_DV, 0, 0, n_p),
            rows(GLA_DV, 0, n_p, n_s),
            _resident((GLA_DV, D_MODEL)),
            rows(D_MODEL, 0, 0, n_p),
            rows(D_MODEL, 0, n_p, n_s),
        ],
        out_specs=rows(D_MODEL, 0, 0, n),
        out_shape=jax.ShapeDtypeStruct((n * TM_OUT, D_MODEL), F32),
        compiler_params=_cparams(("arbitrary",)),
        name="outproj0",
    )(og_p, og_s, w, xp, xs)


def _time_rows(start, n):
    return pl.ds(ROW_STRIDE * start, n, stride=ROW_STRIDE)


def _conv_front_kernel(x_ref, g_ref, w_ref, b_ref, cw_ref, cb_ref, st_ref, sz_ref, c_ref, sop_ref, sos_ref,
                       h_cur, h_nxt, u_stage, buf, hist, *, n_p, tiles_per_seq, tn):
    s = pl.program_id(0)
    nseg = TM // SEG
    region = HALO + SEG
    n_slab = CONV_DIM // LANES
    first = HALO - CONV_PAD

    @pl.when(s == 0)
    def _():
        h_nxt[...] = jnp.zeros_like(h_nxt)
        u_stage[...] = jnp.zeros_like(u_stage)
        hist[...] = jnp.zeros_like(hist)

    h_cur[...] = h_nxt[...]
    t_conv = jnp.maximum(s - 2, 0)
    is_p = t_conv < n_p
    for g in range(nseg):
        for sl in range(n_slab):
            lanes = slice(sl * LANES, (sl + 1) * LANES)
            buf[sl, _time_rows(g * region + HALO, SEG), :] = u_stage[g * SEG:(g + 1) * SEG, lanes]

    @pl.when(is_p)
    def _():
        @pl.when(lax.rem(t_conv, tiles_per_seq) == 0)
        def _():
            hist[...] = jnp.zeros_like(hist)

        for sl in range(n_slab):
            lanes = slice(sl * LANES, (sl + 1) * LANES)
            buf[sl, _time_rows(0, HALO), :] = hist[:, lanes]
            for g in range(1, nseg):
                buf[sl, _time_rows(g * region, HALO), :] = u_stage[g * SEG - HALO:g * SEG, lanes]
        tail = u_stage[TM - HALO:TM, :]
        hist[...] = tail

        @pl.when(lax.rem(t_conv, tiles_per_seq) == tiles_per_seq - 1)
        def _():
            sop_ref[t_conv // tiles_per_seq] = tail

    @pl.when(jnp.logical_not(is_p))
    def _():
        for g in range(nseg):
            seq_i = (t_conv - n_p) * nseg + g
            for sl in range(n_slab):
                lanes = slice(sl * LANES, (sl + 1) * LANES)
                buf[sl, _time_rows(g * region, HALO), :] = st_ref[seq_i, :, lanes]
            sos_ref[seq_i] = u_stage[(g + 1) * SEG - HALO:(g + 1) * SEG, :]

    n_chunks = CONV_DIM // tn
    slabs_per_chunk = n_slab // n_chunks
    prev_acc = None
    for j in range(n_chunks):
        n0 = j * tn
        cols = slice(n0, n0 + tn)
        gcols = slice(CONV_DIM + n0, CONV_DIM + n0 + tn)
        zcols = slice(2 * CONV_DIM + n0, 2 * CONV_DIM + n0 + tn)
        a = jnp.dot(h_cur[...], w_ref[:, cols], preferred_element_type=F32) + b_ref[:, cols]
        ga = jnp.dot(h_cur[...], w_ref[:, gcols], preferred_element_type=F32) + b_ref[:, gcols]
        u_stage[:, cols] = a * jax.nn.sigmoid(ga)
        z = jnp.dot(h_cur[...], w_ref[:, zcols], preferred_element_type=F32) + b_ref[:, zcols]
        sz_ref[:, cols] = _silu(z).astype(BF16)

        for sl in range(j * slabs_per_chunk, (j + 1) * slabs_per_chunk):
            lanes = slice(sl * LANES, (sl + 1) * LANES)
            for g in range(nseg):
                acc = jnp.broadcast_to(cb_ref[:, lanes], (SEG, LANES))
                if prev_acc is not None:
                    acc = _after(acc, prev_acc)
                for kk in range(CONV_WIDTH):
                    acc = acc + buf[sl, _time_rows(g * region + first + kk, SEG), :] * cw_ref[kk:kk + 1, lanes]
                c_ref[g * SEG:(g + 1) * SEG, lanes] = acc
                prev_acc = acc

    h_nxt[...] = _rmsnorm_rows(x_ref[...], g_ref[...]).astype(BF16)


def _conv_front(x1, g, w, b, cw, cb, st, n_p, n_prompt_seq, tn=512):
    n = x1.shape[0] // TM
    n_stream = st.shape[0]
    return pl.pallas_call(
        functools.partial(_conv_front_kernel, n_p=n_p, tiles_per_seq=n_p // n_prompt_seq, tn=tn),
        grid=(n + 2,),
        in_specs=[
            _tile_spec(D_MODEL, 0, 0, n),
            _resident((1, D_MODEL)),
            _resident((D_MODEL, 3 * CONV_DIM)),
            _resident((1, 3 * CONV_DIM)),
            _resident((CONV_WIDTH, CONV_DIM)),
            _resident((1, CONV_DIM)),
            _resident((n_stream, HALO, CONV_DIM)),
        ],
        out_specs=[
            _tile_spec(CONV_DIM, 1, 0, n),
            _tile_spec(CONV_DIM, 2, 0, n),
            pl.BlockSpec((n_prompt_seq, HALO, CONV_DIM), lambda s: (0, 0, 0)),
            pl.BlockSpec((n_stream, HALO, CONV_DIM), lambda s: (0, 0, 0)),
        ],
        out_shape=[
            jax.ShapeDtypeStruct((n * TM, CONV_DIM), BF16),
            jax.ShapeDtypeStruct((n * TM, CONV_DIM), F32),
            jax.ShapeDtypeStruct((n_prompt_seq, HALO, CONV_DIM), F32),
            jax.ShapeDtypeStruct((n_stream, HALO, CONV_DIM), F32),
        ],
        scratch_shapes=[
            pltpu.VMEM((TM, D_MODEL), BF16),
            pltpu.VMEM((TM, D_MODEL), BF16),
            pltpu.VMEM((TM, CONV_DIM), F32),
            pltpu.VMEM((CONV_DIM // LANES, ROW_STRIDE * (TM // SEG) * (HALO + SEG), LANES), F32),
            pltpu.VMEM((HALO, CONV_DIM), F32),
        ],
        compiler_params=_cparams(("arbitrary",)),
        name="conv_front",
    )(x1, g, w, b, cw, cb, st)


def _outproj1_kernel(c_ref, sz_ref, lg_ref, lb_ref, w_ref, x_ref, fg_ref, yp_ref, ys_ref, a_scr, y_scr,
                     *, n_p, tn):
    c = c_ref[...]
    mu = jnp.mean(c, axis=-1, keepdims=True)
    xc = c - mu
    ln = xc * lax.rsqrt(jnp.mean(xc * xc, axis=-1, keepdims=True) + EPS) * lg_ref[...] + lb_ref[...]
    a_scr[...] = (_silu(ln) * sz_ref[...].astype(F32)).astype(BF16)

    for n0 in range(0, D_MODEL, tn):
        cols = slice(n0, n0 + tn)
        y_scr[:, cols] = x_ref[:, cols] + jnp.dot(a_scr[...], w_ref[:, cols], preferred_element_type=F32)
    y = y_scr[...]
    scale = lax.rsqrt(jnp.mean(y * y, axis=-1, keepdims=True) + EPS)
    is_p = pl.program_id(0) < n_p

    @pl.when(is_p)
    def _():
        yp_ref[...] = y_scr[...] * scale * fg_ref[...]

    @pl.when(jnp.logical_not(is_p))
    def _():
        ys_ref[...] = y_scr[...] * scale * fg_ref[...]


def _outproj1(c, sz, lg, lb, w, x1, fg, n_p_rows, tn=512):
    n = x1.shape[0] // TM_OUT
    n_p = n_p_rows // TM_OUT
    n_s = n - n_p
    vec = _resident((1, D_MODEL))
    rows = functools.partial(_tile_spec, tm=TM_OUT)
    return pl.pallas_call(
        functools.partial(_outproj1_kernel, n_p=n_p, tn=tn),
        grid=(n,),
        in_specs=[
            rows(CONV_DIM, 0, 0, n),
            rows(CONV_DIM, 0, 0, n),
            vec, vec,
            _resident((CONV_DIM, D_MODEL)),
            rows(D_MODEL, 0, 0, n),
            vec,
        ],
        out_specs=[rows(D_MODEL, 0, 0, n_p), rows(D_MODEL, 0, n_p, n_s)],
        out_shape=[
            jax.ShapeDtypeStruct((n_p * TM_OUT, D_MODEL), F32),
            jax.ShapeDtypeStruct((n_s * TM_OUT, D_MODEL), F32),
        ],
        scratch_shapes=[pltpu.VMEM((TM_OUT, CONV_DIM), BF16), pltpu.VMEM((TM_OUT, D_MODEL), F32)],
        compiler_params=_cparams(("arbitrary",)),
        name="outproj1",
    )(c, sz, lg, lb, w, x1, fg)


def _forward(x_prompt, x_sample, gla_state, conv_hist, p, tt_prompt=1024):
    n_pseq, p_len, _ = x_prompt.shape
    n_sseq, s_len, _ = x_sample.shape
    assert s_len == SEG and p_len % TM == 0 and (n_sseq * s_len) % TM == 0
    xp = x_prompt.reshape(n_pseq * p_len, D_MODEL)
    xs = x_sample.reshape(n_sseq * s_len, D_MODEL)
    n_p = xp.shape[0] // TM

    qkvr, loga = _gla_front(xp, xs, p["norm_g0"], p["gla_w_in_t"], p["gla_wa2"], p["gla_ba"])
    og_p, gla_p = _gla_core(qkvr, loga, p["gla_gn"], None, n_pseq, p_len, 0, tt_prompt)
    og_s, gla_s = _gla_core(qkvr, loga, p["gla_gn"], gla_state, n_sseq, s_len, xp.shape[0], s_len)
    x1 = _outproj0(og_p, og_s, p["gla_w_out"], xp, xs)

    sz, c, conv_p, conv_s = _conv_front(x1, p["norm_g1"], p["conv_w_in"], p["conv_b_in"], p["conv_w"],
                                        p["conv_b"], conv_hist, n_p, n_pseq)
    y_p, y_s = _outproj1(c, sz, p["conv_ln_g"], p["conv_ln_b"], p["conv_w_out"], x1, p["final_g"], xp.shape[0])
    keep = slice(HALO - CONV_PAD, HALO)
    return (y_p.reshape(x_prompt.shape), y_s.reshape(x_sample.shape), gla_p[None], conv_p[None, :, keep, :],
            gla_s[None], conv_s[None, :, keep, :])


def _prepare(norm_g, final_norm_g, gla_w_in, gla_w_a2, gla_b_a, gla_norm_g, gla_w_out, conv_w_in, conv_b_in,
             conv_w, conv_b, conv_ln_g, conv_ln_b, conv_w_out):
    w_in0 = gla_w_in[0]
    return {
        "norm_g0": norm_g[0:1],
        "norm_g1": norm_g[1:2],
        "final_g": final_norm_g[None, :],
        "gla_w_in_t": jnp.swapaxes(w_in0, 0, 1),
        "gla_wa2": jnp.pad(gla_w_a2[0], ((0, LANES - GATE_RANK), (0, 0))).astype(BF16),
        "gla_ba": gla_b_a[0:1],
        "gla_gn": gla_norm_g[0:1],
        "gla_w_out": gla_w_out[0].astype(BF16),
        "conv_w_in": conv_w_in[0].astype(BF16),
        "conv_b_in": conv_b_in[0:1],
        "conv_w": conv_w[0],
        "conv_b": conv_b[0:1],
        "conv_ln_g": conv_ln_g[0:1],
        "conv_ln_b": conv_ln_b[0:1],
        "conv_w_out": conv_w_out[0].astype(BF16),
    }


def kernel(x_prompt, x_sample, state_gla, state_conv, norm_g, final_norm_g, gla_w_in, gla_w_a2, gla_b_a,
           gla_norm_g, gla_w_out, conv_w_in, conv_b_in, conv_w, conv_b, conv_ln_g, conv_ln_b, conv_w_out):
    p = _prepare(norm_g, final_norm_g, gla_w_in, gla_w_a2, gla_b_a, gla_norm_g, gla_w_out, conv_w_in, conv_b_in,
                 conv_w, conv_b, conv_ln_g, conv_ln_b, conv_w_out)
    conv_hist = jnp.pad(state_conv[0], ((0, 0), (HALO - CONV_PAD, 0), (0, 0)))
    return _forward(x_prompt, x_sample, state_gla[0], conv_hist, p)
```

```python
import functools

import jax
import jax.numpy as jnp
from jax import lax
from jax.experimental import pallas as pl
from jax.experimental.pallas import tpu as pltpu

F32 = jnp.float32
BF16 = jnp.bfloat16

D_MODEL = 2048
CHUNK = 64
GLA_HEADS = 4
GLA_DK = 1024
GLA_DV = 2048
DK_HEAD = GLA_DK // GLA_HEADS
DV_HEAD = GLA_DV // GLA_HEADS
GATE_RANK = 16
GATE_NORM = 16.0
GLA_MAIN = 2 * GLA_DK + 2 * GLA_DV
CONV_DIM = 2048
CONV_WIDTH = 31
CONV_PAD = CONV_WIDTH - 1
EPS = 1e-6
LOG2E = 1.4426950408889634

SUBLANES = 8
LANES = 128
HALO = 32
ROW_STRIDE = 2
TM = 256
TM_OUT = 512
W_CHUNK = 256
W_SLOTS = 4
W_ROWS_IN = 32
W_ROWS_OUT = 128
SEG = 64
VMEM_LIMIT = 56 * 1024 * 1024


def _cparams(sem):
    return pltpu.CompilerParams(dimension_semantics=sem, vmem_limit_bytes=VMEM_LIMIT)


def _after(x, dep):
    half = jnp.uint32(16)
    zero = lax.shift_right_logical(lax.shift_right_logical(lax.bitcast_convert_type(dep, jnp.uint32), half), half)
    return lax.bitcast_convert_type(lax.bitcast_convert_type(x, jnp.uint32) | zero, F32)


def _silu(x):
    return x * jax.nn.sigmoid(x)


def _rmsnorm_rows(x, g):
    ms = jnp.mean(x * x, axis=-1, keepdims=True)
    return x * lax.rsqrt(ms + EPS) * g


def _stream_rows_bf16(w_hbm, w_scr, stage, sem, rows):
    n_slots = stage.shape[0]
    n_w = w_scr.shape[0] // rows

    def chunk(j):
        slot = j % n_slots
        return pltpu.make_async_copy(w_hbm.at[pl.ds(j * rows, rows), :], stage.at[slot], sem.at[slot])

    for j in range(n_slots - 1):
        chunk(j).start()
    for j in range(n_w):
        if j + n_slots - 1 < n_w:
            chunk(j + n_slots - 1).start()
        chunk(j).wait()
        w_scr[j * rows:(j + 1) * rows, :] = stage[j % n_slots].astype(BF16)


def _resident(shape):
    return pl.BlockSpec(shape, lambda *_: (0,) * len(shape), pipeline_mode=pl.Buffered(1))


def _tile_spec(width, lag, first, count, tm=TM):
    return pl.BlockSpec((tm, width), lambda s: (jnp.clip(s - lag - first, 0, count - 1), 0))


def _gla_front_kernel(xp_ref, xs_ref, g_ref, w_hbm, wa2_ref, ba_ref, o_ref, la_ref, h_cur, h_nxt,
                      w_ref, wa1t_ref, stage, sem, *, n_p, tn):
    s = pl.program_id(0)

    def weight_chunk(j):
        slot = j % W_SLOTS
        return pltpu.make_async_copy(w_hbm.at[pl.ds(j * W_CHUNK, W_CHUNK), :], stage.at[slot], sem.at[slot])

    @pl.when(s == 0)
    def _():
        h_nxt[...] = jnp.zeros_like(h_nxt)
        n_w = GLA_MAIN // W_CHUNK
        for j in range(W_SLOTS - 1):
            weight_chunk(j).start()
        for j in range(n_w):
            if j + W_SLOTS - 1 < n_w:
                weight_chunk(j + W_SLOTS - 1).start()
            weight_chunk(j).wait()
            w_ref[:, j * W_CHUNK:(j + 1) * W_CHUNK] = jnp.transpose(stage[j % W_SLOTS]).astype(BF16)
        gate_rows = pltpu.make_async_copy(w_hbm.at[pl.ds(GLA_MAIN, GATE_RANK), :],
                                          stage.at[0, pl.ds(0, GATE_RANK), :], sem.at[0])
        gate_rows.start()
        gate_rows.wait()
        wa1t_ref[...] = jnp.zeros_like(wa1t_ref)
        wa1t_ref[0:GATE_RANK, :] = stage[0, 0:GATE_RANK, :].astype(BF16)

    h_cur[...] = h_nxt[...]

    h = h_cur[...]
    a1 = lax.dot_general(h, wa1t_ref[...], (((1,), (1,)), ((), ())), preferred_element_type=F32)
    z = jnp.dot(a1.astype(BF16), wa2_ref[...], preferred_element_type=F32) + ba_ref[...]
    la_ref[...] = (jnp.minimum(z, 0.0) - jnp.log(1.0 + jnp.exp(-jnp.abs(z)))) * (1.0 / GATE_NORM)
    for n0 in range(0, GLA_MAIN, tn):
        cols = slice(n0, n0 + tn)
        o_ref[:, cols] = jnp.dot(h_cur[...], w_ref[:, cols], preferred_element_type=F32).astype(BF16)

    x = jnp.where(s < n_p, xp_ref[...], xs_ref[...])
    h_nxt[...] = _rmsnorm_rows(x, g_ref[...]).astype(BF16)


def _gla_front(xp, xs, g, w_in_f32, wa2p, ba, tn=512):
    n_p, n_s = xp.shape[0] // TM, xs.shape[0] // TM
    n = n_p + n_s
    return pl.pallas_call(
        functools.partial(_gla_front_kernel, n_p=n_p, tn=tn),
        grid=(n + 1,),
        in_specs=[
            _tile_spec(D_MODEL, 0, 0, n_p),
            _tile_spec(D_MODEL, 0, n_p, n_s),
            _resident((1, D_MODEL)),
            pl.BlockSpec(memory_space=pl.ANY),
            _resident((LANES, GLA_DK)),
            _resident((1, GLA_DK)),
        ],
        out_specs=[_tile_spec(GLA_MAIN, 1, 0, n), _tile_spec(GLA_DK, 1, 0, n)],
        out_shape=[
            jax.ShapeDtypeStruct((n * TM, GLA_MAIN), BF16),
            jax.ShapeDtypeStruct((n * TM, GLA_DK), F32),
        ],
        scratch_shapes=[
            pltpu.VMEM((TM, D_MODEL), BF16),
            pltpu.VMEM((TM, D_MODEL), BF16),
            pltpu.VMEM((D_MODEL, GLA_MAIN), BF16),
            pltpu.VMEM((LANES, D_MODEL), BF16),
            pltpu.VMEM((W_SLOTS, W_CHUNK, D_MODEL), F32),
            pltpu.SemaphoreType.DMA((W_SLOTS,)),
        ],
        compiler_params=_cparams(("arbitrary",)),
        name="gla_front",
    )(xp, xs, g, w_in_f32, wa2p, ba)


def _chunk_cumsum(la_ref, r0):
    sub = lax.broadcasted_iota(jnp.int32, (SUBLANES, GLA_DK), 0)
    groups = []
    carry = None
    for gi in range(CHUNK // SUBLANES):
        x = la_ref[pl.ds(r0 + gi * SUBLANES, SUBLANES), :]
        for s in (1, 2, 4):
            x = x + jnp.where(sub >= s, pltpu.roll(x, s, 0), 0.0)
        if carry is not None:
            x = x + carry
        carry = x[SUBLANES - 1:SUBLANES, :]
        groups.append(x)
    return jnp.concatenate(groups, axis=0)


def _gla_core_kernel(*refs, n_chunks, has_state):
    if has_state:
        q_ref, k_ref, v_ref, r_ref, la_ref, gn_ref, s0_ref, og_ref, so_ref, s_ref = refs
    else:
        q_ref, k_ref, v_ref, r_ref, la_ref, gn_ref, og_ref, so_ref, s_ref = refs
    t = pl.program_id(1)

    @pl.when(t == 0)
    def _():
        if has_state:
            s_ref[...] = s0_ref[0]
        else:
            s_ref[...] = jnp.zeros_like(s_ref)

    row = lax.broadcasted_iota(jnp.int32, (CHUNK, CHUNK), 0)
    col = lax.broadcasted_iota(jnp.int32, (CHUNK, CHUNK), 1)
    causal = row >= col

    def chunk_body(c, carry):
        r0 = pl.multiple_of(c * CHUNK, CHUNK)
        rows = pl.ds(r0, CHUNK)
        b2 = _chunk_cumsum(la_ref, r0) * LOG2E
        bl2 = b2[CHUNK - 1:CHUNK, :]
        q = q_ref[rows, :].astype(F32)
        k = k_ref[rows, :].astype(F32)
        qe = (q * (jnp.exp2(b2) * (DK_HEAD ** -0.5))).astype(BF16)
        ke = (k * jnp.exp2(-b2)).astype(BF16)
        kd_t = jnp.transpose(k * jnp.exp2(bl2 - b2)).astype(BF16)
        dec_t = jnp.transpose(jnp.broadcast_to(jnp.exp2(bl2), (SUBLANES, GLA_DK)))
        atts = []
        for h in range(GLA_HEADS):
            ks = slice(h * DK_HEAD, (h + 1) * DK_HEAD)
            att = lax.dot_general(qe[:, ks], ke[:, ks], (((1,), (1,)), ((), ())), preferred_element_type=F32)
            atts.append(jnp.where(causal, att, 0.0).astype(BF16))
        for h in range(GLA_HEADS):
            ks = slice(h * DK_HEAD, (h + 1) * DK_HEAD)
            vs = slice(h * DV_HEAD, (h + 1) * DV_HEAD)
            qh = qe[:, ks]
            vh = v_ref[rows, vs]
            s_old = s_ref[h]
            o = jnp.dot(qh, s_old.astype(BF16), preferred_element_type=F32) + jnp.dot(
                atts[h], vh, preferred_element_type=F32)
            s_ref[h] = dec_t[ks, 0:1] * s_old + jnp.dot(kd_t[ks, :], vh, preferred_element_type=F32)
            on = _rmsnorm_rows(o, gn_ref[:, vs])
            gate = _silu(r_ref[rows, vs].astype(F32))
            og_ref[rows, vs] = (on * gate).astype(BF16)
        return carry

    lax.fori_loop(0, n_chunks, chunk_body, 0, unroll=2 if n_chunks % 2 == 0 else 1)

    @pl.when(t == pl.num_programs(1) - 1)
    def _():
        so_ref[0] = s_ref[...]


def _gla_core(qkvr, loga, gnorm, s0, n_seq, seq_len, row0, tt):
    has_state = s0 is not None
    tps = seq_len // tt
    blk0 = row0 // tt

    def rows(width, col_blk):
        return pl.BlockSpec((tt, width), lambda b, t: (blk0 + b * tps + t, col_blk))

    in_specs = [rows(GLA_DK, 0), rows(GLA_DK, 1), rows(GLA_DV, 1), rows(GLA_DV, 2), rows(GLA_DK, 0),
                pl.BlockSpec((1, GLA_DV), lambda b, t: (0, 0))]
    args = [qkvr, qkvr, qkvr, qkvr, loga, gnorm]
    state_spec = pl.BlockSpec((1, GLA_HEADS, DK_HEAD, DV_HEAD), lambda b, t: (b, 0, 0, 0))
    if has_state:
        in_specs.append(state_spec)
        args.append(s0)
    return pl.pallas_call(
        functools.partial(_gla_core_kernel, n_chunks=tt // CHUNK, has_state=has_state),
        grid=(n_seq, tps),
        in_specs=in_specs,
        out_specs=[pl.BlockSpec((tt, GLA_DV), lambda b, t: (b * tps + t, 0)), state_spec],
        out_shape=[
            jax.ShapeDtypeStruct((n_seq * seq_len, GLA_DV), BF16),
            jax.ShapeDtypeStruct((n_seq, GLA_HEADS, DK_HEAD, DV_HEAD), F32),
        ],
        scratch_shapes=[pltpu.VMEM((GLA_HEADS, DK_HEAD, DV_HEAD), F32)],
        compiler_params=_cparams(("arbitrary", "arbitrary")),
        name="gla_core",
    )(*args)


def _outproj0_kernel(ap_ref, as_ref, w_hbm, xp_ref, xs_ref, o_ref, w_ref, stage, sem, *, n_p, tn):
    @pl.when(pl.program_id(0) == 0)
    def _():
        _stream_rows_bf16(w_hbm, w_ref, stage, sem, W_ROWS_OUT)

    is_p = pl.program_id(0) < n_p
    a = jnp.where(is_p, ap_ref[...], as_ref[...])
    for n0 in range(0, D_MODEL, tn):
        cols = slice(n0, n0 + tn)
        x = jnp.where(is_p, xp_ref[:, cols], xs_ref[:, cols])
        o_ref[:, cols] = x + jnp.dot(a, w_ref[:, cols], preferred_element_type=F32)


def _outproj0(og_p, og_s, w, xp, xs, tn=512):
    n_p, n_s = xp.shape[0] // TM_OUT, xs.shape[0] // TM_OUT
    n = n_p + n_s
    rows = functools.partial(_tile_spec, tm=TM_OUT)
    return pl.pallas_call(
        functools.partial(_outproj0_kernel, n_p=n_p, tn=tn),
        grid=(n,),
        in_specs=[
            rows(GLA_DV, 0, 0, n_p),
            rows(GLA_DV, 0, n_p, n_s),
            pl.BlockSpec(memory_space=pl.ANY),
            rows(D_MODEL, 0, 0, n_p),
            rows(D_MODEL, 0, n_p, n_s),
        ],
        out_specs=rows(D_MODEL, 0, 0, n),
        out_shape=jax.ShapeDtypeStruct((n * TM_OUT, D_MODEL), F32),
        scratch_shapes=[
            pltpu.VMEM((GLA_DV, D_MODEL), BF16),
            pltpu.VMEM((W_SLOTS, W_ROWS_OUT, D_MODEL), F32),
            pltpu.SemaphoreType.DMA((W_SLOTS,)),
        ],
        compiler_params=_cparams(("arbitrary",)),
        name="outproj0",
    )(og_p, og_s, w, xp, xs)


def _time_rows(start, n):
    return pl.ds(ROW_STRIDE * start, n, stride=ROW_STRIDE)


def _conv_front_kernel(x_ref, g_ref, w_hbm, b_ref, cw_ref, cb_ref, st_ref, sz_ref, c_ref, sop_ref, sos_ref,
                       h_cur, h_nxt, u_stage, buf, hist, w_ref, stage, sem, *, n_p, tiles_per_seq, tn):
    s = pl.program_id(0)
    nseg = TM // SEG
    region = HALO + SEG
    n_slab = CONV_DIM // LANES
    first = HALO - CONV_PAD

    @pl.when(s == 0)
    def _():
        h_nxt[...] = jnp.zeros_like(h_nxt)
        u_stage[...] = jnp.zeros_like(u_stage)
        hist[...] = jnp.zeros_like(hist)
        _stream_rows_bf16(w_hbm, w_ref, stage, sem, W_ROWS_IN)

    h_cur[...] = h_nxt[...]
    t_conv = jnp.maximum(s - 2, 0)
    is_p = t_conv < n_p
    for g in range(nseg):
        for sl in range(n_slab):
            lanes = slice(sl * LANES, (sl + 1) * LANES)
            buf[sl, _time_rows(g * region + HALO, SEG), :] = u_stage[g * SEG:(g + 1) * SEG, lanes]

    @pl.when(is_p)
    def _():
        @pl.when(lax.rem(t_conv, tiles_per_seq) == 0)
        def _():
            hist[...] = jnp.zeros_like(hist)

        for sl in range(n_slab):
            lanes = slice(sl * LANES, (sl + 1) * LANES)
            buf[sl, _time_rows(0, HALO), :] = hist[:, lanes]
            for g in range(1, nseg):
                buf[sl, _time_rows(g * region, HALO), :] = u_stage[g * SEG - HALO:g * SEG, lanes]
        tail = u_stage[TM - HALO:TM, :]
        hist[...] = tail

        @pl.when(lax.rem(t_conv, tiles_per_seq) == tiles_per_seq - 1)
        def _():
            sop_ref[t_conv // tiles_per_seq] = tail

    @pl.when(jnp.logical_not(is_p))
    def _():
        for g in range(nseg):
            seq_i = (t_conv - n_p) * nseg + g
            for sl in range(n_slab):
                lanes = slice(sl * LANES, (sl + 1) * LANES)
                buf[sl, _time_rows(g * region, HALO), :] = st_ref[seq_i, :, lanes]
            sos_ref[seq_i] = u_stage[(g + 1) * SEG - HALO:(g + 1) * SEG, :]

    n_chunks = CONV_DIM // tn
    slabs_per_chunk = n_slab // n_chunks
    prev_acc = None
    for j in range(n_chunks):
        n0 = j * tn
        cols = slice(n0, n0 + tn)
        gcols = slice(CONV_DIM + n0, CONV_DIM + n0 + tn)
        zcols = slice(2 * CONV_DIM + n0, 2 * CONV_DIM + n0 + tn)
        a = jnp.dot(h_cur[...], w_ref[:, cols], preferred_element_type=F32) + b_ref[:, cols]
        ga = jnp.dot(h_cur[...], w_ref[:, gcols], preferred_element_type=F32) + b_ref[:, gcols]
        u_stage[:, cols] = a * jax.nn.sigmoid(ga)
        z = jnp.dot(h_cur[...], w_ref[:, zcols], preferred_element_type=F32) + b_ref[:, zcols]
        sz_ref[:, cols] = _silu(z).astype(BF16)

        for sl in range(j * slabs_per_chunk, (j + 1) * slabs_per_chunk):
            lanes = slice(sl * LANES, (sl + 1) * LANES)
            for g in range(nseg):
                acc = jnp.broadcast_to(cb_ref[:, lanes], (SEG, LANES))
                if prev_acc is not None:
                    acc = _after(acc, prev_acc)
                for kk in range(CONV_WIDTH):
                    acc = acc + buf[sl, _time_rows(g * region + first + kk, SEG), :] * cw_ref[kk:kk + 1, lanes]
                c_ref[g * SEG:(g + 1) * SEG, lanes] = acc
                prev_acc = acc

    h_nxt[...] = _rmsnorm_rows(x_ref[...], g_ref[...]).astype(BF16)


def _conv_front(x1, g, w, b, cw, cb, st, n_p, n_prompt_seq, tn=512):
    n = x1.shape[0] // TM
    n_stream = st.shape[0]
    return pl.pallas_call(
        functools.partial(_conv_front_kernel, n_p=n_p, tiles_per_seq=n_p // n_prompt_seq, tn=tn),
        grid=(n + 2,),
        in_specs=[
            _tile_spec(D_MODEL, 0, 0, n),
            _resident((1, D_MODEL)),
            pl.BlockSpec(memory_space=pl.ANY),
            _resident((1, 3 * CONV_DIM)),
            _resident((CONV_WIDTH, CONV_DIM)),
            _resident((1, CONV_DIM)),
            _resident((n_stream, HALO, CONV_DIM)),
        ],
        out_specs=[
            _tile_spec(CONV_DIM, 1, 0, n),
            _tile_spec(CONV_DIM, 2, 0, n),
            pl.BlockSpec((n_prompt_seq, HALO, CONV_DIM), lambda s: (0, 0, 0)),
            pl.BlockSpec((n_stream, HALO, CONV_DIM), lambda s: (0, 0, 0)),
        ],
        out_shape=[
            jax.ShapeDtypeStruct((n * TM, CONV_DIM), BF16),
            jax.ShapeDtypeStruct((n * TM, CONV_DIM), F32),
            jax.ShapeDtypeStruct((n_prompt_seq, HALO, CONV_DIM), F32),
            jax.ShapeDtypeStruct((n_stream, HALO, CONV_DIM), F32),
        ],
        scratch_shapes=[
            pltpu.VMEM((TM, D_MODEL), BF16),
            pltpu.VMEM((TM, D_MODEL), BF16),
            pltpu.VMEM((TM, CONV_DIM), F32),
            pltpu.VMEM((CONV_DIM // LANES, ROW_STRIDE * (TM // SEG) * (HALO + SEG), LANES), F32),
            pltpu.VMEM((HALO, CONV_DIM), F32),
            pltpu.VMEM((D_MODEL, 3 * CONV_DIM), BF16),
            pltpu.VMEM((W_SLOTS, W_ROWS_IN, 3 * CONV_DIM), F32),
            pltpu.SemaphoreType.DMA((W_SLOTS,)),
        ],
        compiler_params=_cparams(("arbitrary",)),
        name="conv_front",
    )(x1, g, w, b, cw, cb, st)


def _outproj1_kernel(c_ref, sz_ref, lg_ref, lb_ref, w_ref, x_ref, fg_ref, yp_ref, ys_ref, a_scr, y_scr,
                     *, n_p, tn):
    c = c_ref[...]
    mu = jnp.mean(c, axis=-1, keepdims=True)
    xc = c - mu
    ln = xc * lax.rsqrt(jnp.mean(xc * xc, axis=-1, keepdims=True) + EPS) * lg_ref[...] + lb_ref[...]
    a_scr[...] = (_silu(ln) * sz_ref[...].astype(F32)).astype(BF16)

    for n0 in range(0, D_MODEL, tn):
        cols = slice(n0, n0 + tn)
        y_scr[:, cols] = x_ref[:, cols] + jnp.dot(a_scr[...], w_ref[:, cols], preferred_element_type=F32)
    y = y_scr[...]
    scale = lax.rsqrt(jnp.mean(y * y, axis=-1, keepdims=True) + EPS)
    is_p = pl.program_id(0) < n_p

    @pl.when(is_p)
    def _():
        yp_ref[...] = y_scr[...] * scale * fg_ref[...]

    @pl.when(jnp.logical_not(is_p))
    def _():
        ys_ref[...] = y_scr[...] * scale * fg_ref[...]


def _outproj1(c, sz, lg, lb, w, x1, fg, n_p_rows, tn=512):
    n = x1.shape[0] // TM_OUT
    n_p = n_p_rows // TM_OUT
    n_s = n - n_p
    vec = _resident((1, D_MODEL))
    rows = functools.partial(_tile_spec, tm=TM_OUT)
    return pl.pallas_call(
        functools.partial(_outproj1_kernel, n_p=n_p, tn=tn),
        grid=(n,),
        in_specs=[
            rows(CONV_DIM, 0, 0, n),
            rows(CONV_DIM, 0, 0, n),
            vec, vec,
            _resident((CONV_DIM, D_MODEL)),
            rows(D_MODEL, 0, 0, n),
            vec,
        ],
        out_specs=[rows(D_MODEL, 0, 0, n_p), rows(D_MODEL, 0, n_p, n_s)],
        out_shape=[
            jax.ShapeDtypeStruct((n_p * TM_OUT, D_MODEL), F32),
            jax.ShapeDtypeStruct((n_s * TM_OUT, D_MODEL), F32),
        ],
        scratch_shapes=[pltpu.VMEM((TM_OUT, CONV_DIM), BF16), pltpu.VMEM((TM_OUT, D_MODEL), F32)],
        compiler_params=_cparams(("arbitrary",)),
        name="outproj1",
    )(c, sz, lg, lb, w, x1, fg)


def _forward(x_prompt, x_sample, gla_state, conv_hist, p, tt_prompt=1024):
    n_pseq, p_len, _ = x_prompt.shape
    n_sseq, s_len, _ = x_sample.shape
    assert s_len == SEG and p_len % TM == 0 and (n_sseq * s_len) % TM == 0
    xp = x_prompt.reshape(n_pseq * p_len, D_MODEL)
    xs = x_sample.reshape(n_sseq * s_len, D_MODEL)
    n_p = xp.shape[0] // TM

    qkvr, loga = _gla_front(xp, xs, p["norm_g0"], p["gla_w_in_t"], p["gla_wa2"], p["gla_ba"])
    og_p, gla_p = _gla_core(qkvr, loga, p["gla_gn"], None, n_pseq, p_len, 0, tt_prompt)
    og_s, gla_s = _gla_core(qkvr, loga, p["gla_gn"], gla_state, n_sseq, s_len, xp.shape[0], s_len)
    x1 = _outproj0(og_p, og_s, p["gla_w_out"], xp, xs)

    sz, c, conv_p, conv_s = _conv_front(x1, p["norm_g1"], p["conv_w_in"], p["conv_b_in"], p["conv_w"],
                                        p["conv_b"], conv_hist, n_p, n_pseq)
    y_p, y_s = _outproj1(c, sz, p["conv_ln_g"], p["conv_ln_b"], p["conv_w_out"], x1, p["final_g"], xp.shape[0])
    keep = slice(HALO - CONV_PAD, HALO)
    return (y_p.reshape(x_prompt.shape), y_s.reshape(x_sample.shape), gla_p[None], conv_p[None, :, keep, :],
            gla_s[None], conv_s[None, :, keep, :])


def _prepare(norm_g, final_norm_g, gla_w_in, gla_w_a2, gla_b_a, gla_norm_g, gla_w_out, conv_w_in, conv_b_in,
             conv_w, conv_b, conv_ln_g, conv_ln_b, conv_w_out):
    w_in0 = gla_w_in[0]
    return {
        "norm_g0": norm_g[0:1],
        "norm_g1": norm_g[1:2],
        "final_g": final_norm_g[None, :],
        "gla_w_in_t": jnp.swapaxes(w_in0, 0, 1),
        "gla_wa2": jnp.pad(gla_w_a2[0], ((0, LANES - GATE_RANK), (0, 0))).astype(BF16),
        "gla_ba": gla_b_a[0:1],
        "gla_gn": gla_norm_g[0:1],
        "gla_w_out": gla_w_out[0],
        "conv_w_in": conv_w_in[0],
        "conv_b_in": conv_b_in[0:1],
        "conv_w": conv_w[0],
        "conv_b": conv_b[0:1],
        "conv_ln_g": conv_ln_g[0:1],
        "conv_ln_b": conv_ln_b[0:1],
        "conv_w_out": conv_w_out[0].astype(BF16),
    }


def kernel(x_prompt, x_sample, state_gla, state_conv, norm_g, final_norm_g, gla_w_in, gla_w_a2, gla_b_a,
           gla_norm_g, gla_w_out, conv_w_in, conv_b_in, conv_w, conv_b, conv_ln_g, conv_ln_b, conv_w_out):
    p = _prepare(norm_g, final_norm_g, gla_w_in, gla_w_a2, gla_b_a, gla_norm_g, gla_w_out, conv_w_in, conv_b_in,
                 conv_w, conv_b, conv_ln_g, conv_ln_b, conv_w_out)
    conv_hist = jnp.pad(state_conv[0], ((0, 0), (HALO - CONV_PAD, 0), (0, 0)))
    return _forward(x_prompt, x_sample, state_gla[0], conv_hist, p)
```

```python
import functools

import jax
import jax.numpy as jnp
from jax import lax
from jax.experimental import pallas as pl
from jax.experimental.pallas import tpu as pltpu

F32 = jnp.float32
BF16 = jnp.bfloat16

D_MODEL = 2048
CHUNK = 64
GLA_HEADS = 4
GLA_DK = 1024
GLA_DV = 2048
DK_HEAD = GLA_DK // GLA_HEADS
DV_HEAD = GLA_DV // GLA_HEADS
GATE_RANK = 16
GATE_NORM = 16.0
GLA_MAIN = 2 * GLA_DK + 2 * GLA_DV
CONV_DIM = 2048
CONV_WIDTH = 31
CONV_PAD = CONV_WIDTH - 1
EPS = 1e-6
LOG2E = 1.4426950408889634

SUBLANES = 8
LANES = 128
HALO = 32
ROW_STRIDE = 2
TM = 256
TM_OUT = 512
W_CHUNK = 256
W_SLOTS = 4
W_ROWS_IN = 32
W_ROWS_OUT = 128
SEG = 64
VMEM_LIMIT = 56 * 1024 * 1024


def _cparams(sem):
    return pltpu.CompilerParams(dimension_semantics=sem, vmem_limit_bytes=VMEM_LIMIT)


def _after(x, dep):
    half = jnp.uint32(16)
    zero = lax.shift_right_logical(lax.shift_right_logical(lax.bitcast_convert_type(dep, jnp.uint32), half), half)
    return lax.bitcast_convert_type(lax.bitcast_convert_type(x, jnp.uint32) | zero, F32)


def _silu(x):
    return x * jax.nn.sigmoid(x)


def _rmsnorm_rows(x, g):
    ms = jnp.mean(x * x, axis=-1, keepdims=True)
    return x * lax.rsqrt(ms + EPS) * g


def _stream_rows_bf16(w_hbm, w_scr, stage, sem, rows):
    n_slots = stage.shape[0]
    n_w = w_scr.shape[0] // rows

    def chunk(j):
        slot = j % n_slots
        return pltpu.make_async_copy(w_hbm.at[pl.ds(j * rows, rows), :], stage.at[slot], sem.at[slot])

    for j in range(n_slots - 1):
        chunk(j).start()
    for j in range(n_w):
        if j + n_slots - 1 < n_w:
            chunk(j + n_slots - 1).start()
        chunk(j).wait()
        w_scr[j * rows:(j + 1) * rows, :] = stage[j % n_slots].astype(BF16)


def _resident(shape):
    return pl.BlockSpec(shape, lambda *_: (0,) * len(shape), pipeline_mode=pl.Buffered(1))


def _tile_spec(width, lag, first, count, tm=TM):
    return pl.BlockSpec((tm, width), lambda s: (jnp.clip(s - lag - first, 0, count - 1), 0))


def _gla_front_kernel(xp_ref, xs_ref, g_ref, w_hbm, wa2_ref, ba_ref, o_ref, la_ref, h_cur, h_nxt,
                      w_ref, wa1t_ref, stage, sem, *, n_p, tn):
    s = pl.program_id(0)

    def weight_chunk(j):
        slot = j % W_SLOTS
        return pltpu.make_async_copy(w_hbm.at[pl.ds(j * W_CHUNK, W_CHUNK), :], stage.at[slot], sem.at[slot])

    @pl.when(s == 0)
    def _():
        h_nxt[...] = jnp.zeros_like(h_nxt)
        n_w = GLA_MAIN // W_CHUNK
        for j in range(W_SLOTS - 1):
            weight_chunk(j).start()
        for j in range(n_w):
            if j + W_SLOTS - 1 < n_w:
                weight_chunk(j + W_SLOTS - 1).start()
            weight_chunk(j).wait()
            w_ref[:, j * W_CHUNK:(j + 1) * W_CHUNK] = jnp.transpose(stage[j % W_SLOTS]).astype(BF16)
        gate_rows = pltpu.make_async_copy(w_hbm.at[pl.ds(GLA_MAIN, GATE_RANK), :],
                                          stage.at[0, pl.ds(0, GATE_RANK), :], sem.at[0])
        gate_rows.start()
        gate_rows.wait()
        wa1t_ref[...] = jnp.zeros_like(wa1t_ref)
        wa1t_ref[0:GATE_RANK, :] = stage[0, 0:GATE_RANK, :].astype(BF16)

    h_cur[...] = h_nxt[...]

    h = h_cur[...]
    a1 = lax.dot_general(h, wa1t_ref[...], (((1,), (1,)), ((), ())), preferred_element_type=F32)
    z = jnp.dot(a1.astype(BF16), wa2_ref[...], preferred_element_type=F32) + ba_ref[...]
    la_ref[...] = (jnp.minimum(z, 0.0) - jnp.log(1.0 + jnp.exp(-jnp.abs(z)))) * (1.0 / GATE_NORM)
    for n0 in range(0, GLA_MAIN, tn):
        cols = slice(n0, n0 + tn)
        o_ref[:, cols] = jnp.dot(h_cur[...], w_ref[:, cols], preferred_element_type=F32).astype(BF16)

    x = jnp.where(s < n_p, xp_ref[...], xs_ref[...])
    h_nxt[...] = _rmsnorm_rows(x, g_ref[...]).astype(BF16)


def _gla_front(xp, xs, g, w_in_f32, wa2p, ba, tn=512):
    n_p, n_s = xp.shape[0] // TM, xs.shape[0] // TM
    n = n_p + n_s
    return pl.pallas_call(
        functools.partial(_gla_front_kernel, n_p=n_p, tn=tn),
        grid=(n + 1,),
        in_specs=[
            _tile_spec(D_MODEL, 0, 0, n_p),
            _tile_spec(D_MODEL, 0, n_p, n_s),
            _resident((1, D_MODEL)),
            pl.BlockSpec(memory_space=pl.ANY),
            _resident((LANES, GLA_DK)),
            _resident((1, GLA_DK)),
        ],
        out_specs=[_tile_spec(GLA_MAIN, 1, 0, n), _tile_spec(GLA_DK, 1, 0, n)],
        out_shape=[
            jax.ShapeDtypeStruct((n * TM, GLA_MAIN), BF16),
            jax.ShapeDtypeStruct((n * TM, GLA_DK), F32),
        ],
        scratch_shapes=[
            pltpu.VMEM((TM, D_MODEL), BF16),
            pltpu.VMEM((TM, D_MODEL), BF16),
            pltpu.VMEM((D_MODEL, GLA_MAIN), BF16),
            pltpu.VMEM((LANES, D_MODEL), BF16),
            pltpu.VMEM((W_SLOTS, W_CHUNK, D_MODEL), F32),
            pltpu.SemaphoreType.DMA((W_SLOTS,)),
        ],
        compiler_params=_cparams(("arbitrary",)),
        name="gla_front",
    )(xp, xs, g, w_in_f32, wa2p, ba)


def _chunk_cumsum(la_ref, r0):
    sub = lax.broadcasted_iota(jnp.int32, (SUBLANES, GLA_DK), 0)
    groups = []
    carry = None
    for gi in range(CHUNK // SUBLANES):
        x = la_ref[pl.ds(r0 + gi * SUBLANES, SUBLANES), :]
        for s in (1, 2, 4):
            x = x + jnp.where(sub >= s, pltpu.roll(x, s, 0), 0.0)
        if carry is not None:
            x = x + carry
        carry = x[SUBLANES - 1:SUBLANES, :]
        groups.append(x)
    return jnp.concatenate(groups, axis=0)


def _gla_core_kernel(*refs, n_chunks, has_state):
    if has_state:
        q_ref, k_ref, v_ref, r_ref, la_ref, gn_ref, s0_ref, og_ref, so_ref, s_ref = refs
    else:
        q_ref, k_ref, v_ref, r_ref, la_ref, gn_ref, og_ref, so_ref, s_ref = refs
    t = pl.program_id(1)

    @pl.when(t == 0)
    def _():
        if has_state:
            s_ref[...] = s0_ref[0]
        else:
            s_ref[...] = jnp.zeros_like(s_ref)

    row = lax.broadcasted_iota(jnp.int32, (CHUNK, CHUNK), 0)
    col = lax.broadcasted_iota(jnp.int32, (CHUNK, CHUNK), 1)
    causal = row >= col

    def chunk_body(c, carry):
        r0 = pl.multiple_of(c * CHUNK, CHUNK)
        rows = pl.ds(r0, CHUNK)
        b2 = _chunk_cumsum(la_ref, r0) * LOG2E
        bl2 = b2[CHUNK - 1:CHUNK, :]
        q = q_ref[rows, :].astype(F32)
        k = k_ref[rows, :].astype(F32)
        qe = (q * (jnp.exp2(b2) * (DK_HEAD ** -0.5))).astype(BF16)
        ke = (k * jnp.exp2(-b2)).astype(BF16)
        kd_t = jnp.transpose(k * jnp.exp2(bl2 - b2)).astype(BF16)
        dec_t = jnp.transpose(jnp.broadcast_to(jnp.exp2(bl2), (SUBLANES, GLA_DK)))
        atts = []
        for h in range(GLA_HEADS):
            ks = slice(h * DK_HEAD, (h + 1) * DK_HEAD)
            att = lax.dot_general(qe[:, ks], ke[:, ks], (((1,), (1,)), ((), ())), preferred_element_type=F32)
            atts.append(jnp.where(causal, att, 0.0).astype(BF16))
        for h in range(GLA_HEADS):
            ks = slice(h * DK_HEAD, (h + 1) * DK_HEAD)
            vs = slice(h * DV_HEAD, (h + 1) * DV_HEAD)
            qh = qe[:, ks]
            vh = v_ref[rows, vs]
            s_old = s_ref[h]
            o = jnp.dot(qh, s_old.astype(BF16), preferred_element_type=F32) + jnp.dot(
                atts[h], vh, preferred_element_type=F32)
            s_ref[h] = dec_t[ks, 0:1] * s_old + jnp.dot(kd_t[ks, :], vh, preferred_element_type=F32)
            on = _rmsnorm_rows(o, gn_ref[:, vs])
            gate = _silu(r_ref[rows, vs].astype(F32))
            og_ref[rows, vs] = (on * gate).astype(BF16)
        return carry

    lax.fori_loop(0, n_chunks, chunk_body, 0, unroll=2 if n_chunks % 2 == 0 else 1)

    @pl.when(t == pl.num_programs(1) - 1)
    def _():
        so_ref[0] = s_ref[...]


def _gla_core(qkvr, loga, gnorm, s0, n_seq, seq_len, row0, tt):
    has_state = s0 is not None
    tps = seq_len // tt
    blk0 = row0 // tt

    def rows(width, col_blk):
        return pl.BlockSpec((tt, width), lambda b, t: (blk0 + b * tps + t, col_blk))

    in_specs = [rows(GLA_DK, 0), rows(GLA_DK, 1), rows(GLA_DV, 1), rows(GLA_DV, 2), rows(GLA_DK, 0),
                pl.BlockSpec((1, GLA_DV), lambda b, t: (0, 0))]
    args = [qkvr, qkvr, qkvr, qkvr, loga, gnorm]
    state_spec = pl.BlockSpec((1, GLA_HEADS, DK_HEAD, DV_HEAD), lambda b, t: (b, 0, 0, 0))
    if has_state:
        in_specs.append(state_spec)
        args.append(s0)
    return pl.pallas_call(
        functools.partial(_gla_core_kernel, n_chunks=tt // CHUNK, has_state=has_state),
        grid=(n_seq, tps),
        in_specs=in_specs,
        out_specs=[pl.BlockSpec((tt, GLA_DV), lambda b, t: (b * tps + t, 0)), state_spec],
        out_shape=[
            jax.ShapeDtypeStruct((n_seq * seq_len, GLA_DV), BF16),
            jax.ShapeDtypeStruct((n_seq, GLA_HEADS, DK_HEAD, DV_HEAD), F32),
        ],
        scratch_shapes=[pltpu.VMEM((GLA_HEADS, DK_HEAD, DV_HEAD), F32)],
        compiler_params=_cparams(("arbitrary", "arbitrary")),
        name="gla_core",
    )(*args)


def _outproj0_kernel(ap_ref, as_ref, w_hbm, xp_ref, xs_ref, o_ref, w_ref, stage, sem, *, n_p, tn):
    @pl.when(pl.program_id(0) == 0)
    def _():
        _stream_rows_bf16(w_hbm, w_ref, stage, sem, W_ROWS_OUT)

    is_p = pl.program_id(0) < n_p
    a = jnp.where(is_p, ap_ref[...], as_ref[...])
    for n0 in range(0, D_MODEL, tn):
        cols = slice(n0, n0 + tn)
        x = jnp.where(is_p, xp_ref[:, cols], xs_ref[:, cols])
        o_ref[:, cols] = x + jnp.dot(a, w_ref[:, cols], preferred_element_type=F32)


def _outproj0(og_p, og_s, w, xp, xs, tn=512):
    n_p, n_s = xp.shape[0] // TM_OUT, xs.shape[0] // TM_OUT
    n = n_p + n_s
    rows = functools.partial(_tile_spec, tm=TM_OUT)
    return pl.pallas_call(
        functools.partial(_outproj0_kernel, n_p=n_p, tn=tn),
        grid=(n,),
        in_specs=[
            rows(GLA_DV, 0, 0, n_p),
            rows(GLA_DV, 0, n_p, n_s),
            pl.BlockSpec(memory_space=pl.ANY),
            rows(D_MODEL, 0, 0, n_p),
            rows(D_MODEL, 0, n_p, n_s),
        ],
        out_specs=rows(D_MODEL, 0, 0, n),
        out_shape=jax.ShapeDtypeStruct((n * TM_OUT, D_MODEL), F32),
        scratch_shapes=[
            pltpu.VMEM((GLA_DV, D_MODEL), BF16),
            pltpu.VMEM((W_SLOTS, W_ROWS_OUT, D_MODEL), F32),
            pltpu.SemaphoreType.DMA((W_SLOTS,)),
        ],
        compiler_params=_cparams(("arbitrary",)),
        name="outproj0",
    )(og_p, og_s, w, xp, xs)


def _time_rows(start, n):
    return pl.ds(ROW_STRIDE * start, n, stride=ROW_STRIDE)


def _conv_front_kernel(x_ref, g_ref, w_hbm, b_ref, cw_ref, cb_ref, st_ref, sz_ref, c_ref, sop_ref, sos_ref,
                       h_cur, h_nxt, u_stage, buf, hist, w_ref, stage, sem, *, n_p, tiles_per_seq, tn):
    s = pl.program_id(0)
    nseg = TM // SEG
    region = HALO + SEG
    n_slab = CONV_DIM // LANES
    first = HALO - CONV_PAD

    @pl.when(s == 0)
    def _():
        h_nxt[...] = jnp.zeros_like(h_nxt)
        u_stage[...] = jnp.zeros_like(u_stage)
        hist[...] = jnp.zeros_like(hist)
        _stream_rows_bf16(w_hbm, w_ref, stage, sem, W_ROWS_IN)

    h_cur[...] = h_nxt[...]
    t_conv = jnp.maximum(s - 2, 0)
    is_p = t_conv < n_p
    for g in range(nseg):
        for sl in range(n_slab):
            lanes = slice(sl * LANES, (sl + 1) * LANES)
            buf[sl, _time_rows(g * region + HALO, SEG), :] = u_stage[g * SEG:(g + 1) * SEG, lanes]

    @pl.when(is_p)
    def _():
        @pl.when(lax.rem(t_conv, tiles_per_seq) == 0)
        def _():
            hist[...] = jnp.zeros_like(hist)

        for sl in range(n_slab):
            lanes = slice(sl * LANES, (sl + 1) * LANES)
            buf[sl, _time_rows(0, HALO), :] = hist[:, lanes]
            for g in range(1, nseg):
                buf[sl, _time_rows(g * region, HALO), :] = u_stage[g * SEG - HALO:g * SEG, lanes]
        tail = u_stage[TM - HALO:TM, :]
        hist[...] = tail

        @pl.when(lax.rem(t_conv, tiles_per_seq) == tiles_per_seq - 1)
        def _():
            sop_ref[t_conv // tiles_per_seq] = tail

    @pl.when(jnp.logical_not(is_p))
    def _():
        for g in range(nseg):
            seq_i = (t_conv - n_p) * nseg + g
            for sl in range(n_slab):
                lanes = slice(sl * LANES, (sl + 1) * LANES)
                buf[sl, _time_rows(g * region, HALO), :] = st_ref[seq_i, :, lanes]
            sos_ref[seq_i] = u_stage[(g + 1) * SEG - HALO:(g + 1) * SEG, :]

    n_chunks = CONV_DIM // tn
    slabs_per_chunk = n_slab // n_chunks
    prev_acc = None
    for j in range(n_chunks):
        n0 = j * tn
        cols = slice(n0, n0 + tn)
        gcols = slice(CONV_DIM + n0, CONV_DIM + n0 + tn)
        zcols = slice(2 * CONV_DIM + n0, 2 * CONV_DIM + n0 + tn)
        a = jnp.dot(h_cur[...], w_ref[:, cols], preferred_element_type=F32) + b_ref[:, cols]
        ga = jnp.dot(h_cur[...], w_ref[:, gcols], preferred_element_type=F32) + b_ref[:, gcols]
        u_stage[:, cols] = a * jax.nn.sigmoid(ga)
        z = jnp.dot(h_cur[...], w_ref[:, zcols], preferred_element_type=F32) + b_ref[:, zcols]
        sz_ref[:, cols] = _silu(z).astype(BF16)

        for sl in range(j * slabs_per_chunk, (j + 1) * slabs_per_chunk):
            lanes = slice(sl * LANES, (sl + 1) * LANES)
            for g in range(nseg):
                acc = jnp.broadcast_to(cb_ref[:, lanes], (SEG, LANES))
                if prev_acc is not None:
                    acc = _after(acc, prev_acc)
                for kk in range(CONV_WIDTH):
                    acc = acc + buf[sl, _time_rows(g * region + first + kk, SEG), :] * cw_ref[kk:kk + 1, lanes]
                c_ref[g * SEG:(g + 1) * SEG, lanes] = acc
                prev_acc = acc

    h_nxt[...] = _rmsnorm_rows(x_ref[...], g_ref[...]).astype(BF16)


def _conv_front(x1, g, w, b, cw, cb, st, n_p, n_prompt_seq, tn=1024):
    n = x1.shape[0] // TM
    n_stream = st.shape[0]
    return pl.pallas_call(
        functools.partial(_conv_front_kernel, n_p=n_p, tiles_per_seq=n_p // n_prompt_seq, tn=tn),
        grid=(n + 2,),
        in_specs=[
            _tile_spec(D_MODEL, 0, 0, n),
            _resident((1, D_MODEL)),
            pl.BlockSpec(memory_space=pl.ANY),
            _resident((1, 3 * CONV_DIM)),
            _resident((CONV_WIDTH, CONV_DIM)),
            _resident((1, CONV_DIM)),
            _resident((n_stream, HALO, CONV_DIM)),
        ],
        out_specs=[
            _tile_spec(CONV_DIM, 1, 0, n),
            _tile_spec(CONV_DIM, 2, 0, n),
            pl.BlockSpec((n_prompt_seq, HALO, CONV_DIM), lambda s: (0, 0, 0)),
            pl.BlockSpec((n_stream, HALO, CONV_DIM), lambda s: (0, 0, 0)),
        ],
        out_shape=[
            jax.ShapeDtypeStruct((n * TM, CONV_DIM), BF16),
            jax.ShapeDtypeStruct((n * TM, CONV_DIM), F32),
            jax.ShapeDtypeStruct((n_prompt_seq, HALO, CONV_DIM), F32),
            jax.ShapeDtypeStruct((n_stream, HALO, CONV_DIM), F32),
        ],
        scratch_shapes=[
            pltpu.VMEM((TM, D_MODEL), BF16),
            pltpu.VMEM((TM, D_MODEL), BF16),
            pltpu.VMEM((TM, CONV_DIM), F32),
            pltpu.VMEM((CONV_DIM // LANES, ROW_STRIDE * (TM // SEG) * (HALO + SEG), LANES), F32),
            pltpu.VMEM((HALO, CONV_DIM), F32),
            pltpu.VMEM((D_MODEL, 3 * CONV_DIM), BF16),
            pltpu.VMEM((W_SLOTS, W_ROWS_IN, 3 * CONV_DIM), F32),
            pltpu.SemaphoreType.DMA((W_SLOTS,)),
        ],
        compiler_params=_cparams(("arbitrary",)),
        name="conv_front",
    )(x1, g, w, b, cw, cb, st)


def _outproj1_kernel(c_ref, sz_ref, lg_ref, lb_ref, w_ref, x_ref, fg_ref, yp_ref, ys_ref, a_scr, y_scr,
                     *, n_p, tn):
    c = c_ref[...]
    mu = jnp.mean(c, axis=-1, keepdims=True)
    xc = c - mu
    ln = xc * lax.rsqrt(jnp.mean(xc * xc, axis=-1, keepdims=True) + EPS) * lg_ref[...] + lb_ref[...]
    a_scr[...] = (_silu(ln) * sz_ref[...].astype(F32)).astype(BF16)

    for n0 in range(0, D_MODEL, tn):
        cols = slice(n0, n0 + tn)
        y_scr[:, cols] = x_ref[:, cols] + jnp.dot(a_scr[...], w_ref[:, cols], preferred_element_type=F32)
    y = y_scr[...]
    scale = lax.rsqrt(jnp.mean(y * y, axis=-1, keepdims=True) + EPS)
    is_p = pl.program_id(0) < n_p

    @pl.when(is_p)
    def _():
        yp_ref[...] = y_scr[...] * scale * fg_ref[...]

    @pl.when(jnp.logical_not(is_p))
    def _():
        ys_ref[...] = y_scr[...] * scale * fg_ref[...]


def _outproj1(c, sz, lg, lb, w, x1, fg, n_p_rows, tn=512):
    n = x1.shape[0] // TM_OUT
    n_p = n_p_rows // TM_OUT
    n_s = n - n_p
    vec = _resident((1, D_MODEL))
    rows = functools.partial(_tile_spec, tm=TM_OUT)
    return pl.pallas_call(
        functools.partial(_outproj1_kernel, n_p=n_p, tn=tn),
        grid=(n,),
        in_specs=[
            rows(CONV_DIM, 0, 0, n),
            rows(CONV_DIM, 0, 0, n),
            vec, vec,
            _resident((CONV_DIM, D_MODEL)),
            rows(D_MODEL, 0, 0, n),
            vec,
        ],
        out_specs=[rows(D_MODEL, 0, 0, n_p), rows(D_MODEL, 0, n_p, n_s)],
        out_shape=[
            jax.ShapeDtypeStruct((n_p * TM_OUT, D_MODEL), F32),
            jax.ShapeDtypeStruct((n_s * TM_OUT, D_MODEL), F32),
        ],
        scratch_shapes=[pltpu.VMEM((TM_OUT, CONV_DIM), BF16), pltpu.VMEM((TM_OUT, D_MODEL), F32)],
        compiler_params=_cparams(("arbitrary",)),
        name="outproj1",
    )(c, sz, lg, lb, w, x1, fg)


def _forward(x_prompt, x_sample, gla_state, conv_hist, p, tt_prompt=1024):
    n_pseq, p_len, _ = x_prompt.shape
    n_sseq, s_len, _ = x_sample.shape
    assert s_len == SEG and p_len % TM == 0 and (n_sseq * s_len) % TM == 0
    xp = x_prompt.reshape(n_pseq * p_len, D_MODEL)
    xs = x_sample.reshape(n_sseq * s_len, D_MODEL)
    n_p = xp.shape[0] // TM

    qkvr, loga = _gla_front(xp, xs, p["norm_g0"], p["gla_w_in_t"], p["gla_wa2"], p["gla_ba"])
    og_p, gla_p = _gla_core(qkvr, loga, p["gla_gn"], None, n_pseq, p_len, 0, tt_prompt)
    og_s, gla_s = _gla_core(qkvr, loga, p["gla_gn"], gla_state, n_sseq, s_len, xp.shape[0], s_len)
    x1 = _outproj0(og_p, og_s, p["gla_w_out"], xp, xs)

    sz, c, conv_p, conv_s = _conv_front(x1, p["norm_g1"], p["conv_w_in"], p["conv_b_in"], p["conv_w"],
                                        p["conv_b"], conv_hist, n_p, n_pseq)
    y_p, y_s = _outproj1(c, sz, p["conv_ln_g"], p["conv_ln_b"], p["conv_w_out"], x1, p["final_g"], xp.shape[0])
    keep = slice(HALO - CONV_PAD, HALO)
    return (y_p.reshape(x_prompt.shape), y_s.reshape(x_sample.shape), gla_p[None], conv_p[None, :, keep, :],
            gla_s[None], conv_s[None, :, keep, :])


def _prepare(norm_g, final_norm_g, gla_w_in, gla_w_a2, gla_b_a, gla_norm_g, gla_w_out, conv_w_in, conv_b_in,
             conv_w, conv_b, conv_ln_g, conv_ln_b, conv_w_out):
    w_in0 = gla_w_in[0]
    return {
        "norm_g0": norm_g[0:1],
        "norm_g1": norm_g[1:2],
        "final_g": final_norm_g[None, :],
        "gla_w_in_t": jnp.swapaxes(w_in0, 0, 1),
        "gla_wa2": jnp.pad(gla_w_a2[0], ((0, LANES - GATE_RANK), (0, 0))).astype(BF16),
        "gla_ba": gla_b_a[0:1],
        "gla_gn": gla_norm_g[0:1],
        "gla_w_out": gla_w_out[0],
        "conv_w_in": conv_w_in[0],
        "conv_b_in": conv_b_in[0:1],
        "conv_w": conv_w[0],
        "conv_b": conv_b[0:1],
        "conv_ln_g": conv_ln_g[0:1],
        "conv_ln_b": conv_ln_b[0:1],
        "conv_w_out": conv_w_out[0].astype(BF16),
    }


def kernel(x_prompt, x_sample, state_gla, state_conv, norm_g, final_norm_g, gla_w_in, gla_w_a2, gla_b_a,
           gla_norm_g, gla_w_out, conv_w_in, conv_b_in, conv_w, conv_b, conv_ln_g, conv_ln_b, conv_w_out):
    p = _prepare(norm_g, final_norm_g, gla_w_in, gla_w_a2, gla_b_a, gla_norm_g, gla_w_out, conv_w_in, conv_b_in,
                 conv_w, conv_b, conv_ln_g, conv_ln_b, conv_w_out)
    conv_hist = jnp.pad(state_conv[0], ((0, 0), (HALO - CONV_PAD, 0), (0, 0)))
    return _forward(x_prompt, x_sample, state_gla[0], conv_hist, p)
```

```python
import functools

import jax
import jax.numpy as jnp
from jax import lax
from jax.experimental import pallas as pl
from jax.experimental.pallas import tpu as pltpu

F32 = jnp.float32
BF16 = jnp.bfloat16

D_MODEL = 2048
CHUNK = 64
GLA_HEADS = 4
GLA_DK = 1024
GLA_DV = 2048
DK_HEAD = GLA_DK // GLA_HEADS
DV_HEAD = GLA_DV // GLA_HEADS
GATE_RANK = 16
GATE_NORM = 16.0
GLA_MAIN = 2 * GLA_DK + 2 * GLA_DV
CONV_DIM = 2048
CONV_WIDTH = 31
CONV_PAD = CONV_WIDTH - 1
EPS = 1e-6
LOG2E = 1.4426950408889634

SUBLANES = 8
LANES = 128
HALO = 32
ROW_STRIDE = 2
TM = 256
TM_OUT = 512
W_CHUNK = 256
W_SLOTS = 4
W_ROWS_IN = 32
W_ROWS_OUT = 128
W_ROWS_OUT1 = 64
SEG = 64
VMEM_LIMIT = 56 * 1024 * 1024


def _cparams(sem):
    return pltpu.CompilerParams(dimension_semantics=sem, vmem_limit_bytes=VMEM_LIMIT)


def _after(x, dep):
    half = jnp.uint32(16)
    zero = lax.shift_right_logical(lax.shift_right_logical(lax.bitcast_convert_type(dep, jnp.uint32), half), half)
    return lax.bitcast_convert_type(lax.bitcast_convert_type(x, jnp.uint32) | zero, F32)


def _silu(x):
    return x * jax.nn.sigmoid(x)


def _rmsnorm_rows(x, g):
    ms = jnp.mean(x * x, axis=-1, keepdims=True)
    return x * lax.rsqrt(ms + EPS) * g


def _stream_rows_bf16(w_hbm, w_scr, stage, sem, rows):
    n_slots = stage.shape[0]
    n_w = w_scr.shape[0] // rows

    def chunk(j):
        slot = j % n_slots
        return pltpu.make_async_copy(w_hbm.at[pl.ds(j * rows, rows), :], stage.at[slot], sem.at[slot])

    for j in range(n_slots - 1):
        chunk(j).start()
    for j in range(n_w):
        if j + n_slots - 1 < n_w:
            chunk(j + n_slots - 1).start()
        chunk(j).wait()
        w_scr[j * rows:(j + 1) * rows, :] = stage[j % n_slots].astype(BF16)


def _resident(shape):
    return pl.BlockSpec(shape, lambda *_: (0,) * len(shape), pipeline_mode=pl.Buffered(1))


def _tile_spec(width, lag, first, count, tm=TM):
    return pl.BlockSpec((tm, width), lambda s: (jnp.clip(s - lag - first, 0, count - 1), 0))


def _gla_front_kernel(xp_ref, xs_ref, g_ref, w_hbm, wa2_ref, ba_ref, o_ref, la_ref, h_cur, h_nxt,
                      w_ref, wa1t_ref, stage, sem, *, n_p, tn):
    s = pl.program_id(0)

    def weight_chunk(j):
        slot = j % W_SLOTS
        return pltpu.make_async_copy(w_hbm.at[pl.ds(j * W_CHUNK, W_CHUNK), :], stage.at[slot], sem.at[slot])

    @pl.when(s == 0)
    def _():
        h_nxt[...] = jnp.zeros_like(h_nxt)
        n_w = GLA_MAIN // W_CHUNK
        for j in range(W_SLOTS - 1):
            weight_chunk(j).start()
        for j in range(n_w):
            if j + W_SLOTS - 1 < n_w:
                weight_chunk(j + W_SLOTS - 1).start()
            weight_chunk(j).wait()
            w_ref[:, j * W_CHUNK:(j + 1) * W_CHUNK] = jnp.transpose(stage[j % W_SLOTS]).astype(BF16)
        gate_rows = pltpu.make_async_copy(w_hbm.at[pl.ds(GLA_MAIN, GATE_RANK), :],
                                          stage.at[0, pl.ds(0, GATE_RANK), :], sem.at[0])
        gate_rows.start()
        gate_rows.wait()
        wa1t_ref[...] = jnp.zeros_like(wa1t_ref)
        wa1t_ref[0:GATE_RANK, :] = stage[0, 0:GATE_RANK, :].astype(BF16)

    h_cur[...] = h_nxt[...]

    h = h_cur[...]
    a1 = lax.dot_general(h, wa1t_ref[...], (((1,), (1,)), ((), ())), preferred_element_type=F32)
    z = jnp.dot(a1.astype(BF16), wa2_ref[...], preferred_element_type=F32) + ba_ref[...]
    la_ref[...] = (jnp.minimum(z, 0.0) - jnp.log(1.0 + jnp.exp(-jnp.abs(z)))) * (1.0 / GATE_NORM)
    for n0 in range(0, GLA_MAIN, tn):
        cols = slice(n0, n0 + tn)
        o_ref[:, cols] = jnp.dot(h_cur[...], w_ref[:, cols], preferred_element_type=F32).astype(BF16)

    x = jnp.where(s < n_p, xp_ref[...], xs_ref[...])
    h_nxt[...] = _rmsnorm_rows(x, g_ref[...]).astype(BF16)


def _gla_front(xp, xs, g, w_in_f32, wa2p, ba, tn=512):
    n_p, n_s = xp.shape[0] // TM, xs.shape[0] // TM
    n = n_p + n_s
    return pl.pallas_call(
        functools.partial(_gla_front_kernel, n_p=n_p, tn=tn),
        grid=(n + 1,),
        in_specs=[
            _tile_spec(D_MODEL, 0, 0, n_p),
            _tile_spec(D_MODEL, 0, n_p, n_s),
            _resident((1, D_MODEL)),
            pl.BlockSpec(memory_space=pl.ANY),
            _resident((LANES, GLA_DK)),
            _resident((1, GLA_DK)),
        ],
        out_specs=[_tile_spec(GLA_MAIN, 1, 0, n), _tile_spec(GLA_DK, 1, 0, n)],
        out_shape=[
            jax.ShapeDtypeStruct((n * TM, GLA_MAIN), BF16),
            jax.ShapeDtypeStruct((n * TM, GLA_DK), F32),
        ],
        scratch_shapes=[
            pltpu.VMEM((TM, D_MODEL), BF16),
            pltpu.VMEM((TM, D_MODEL), BF16),
            pltpu.VMEM((D_MODEL, GLA_MAIN), BF16),
            pltpu.VMEM((LANES, D_MODEL), BF16),
            pltpu.VMEM((W_SLOTS, W_CHUNK, D_MODEL), F32),
            pltpu.SemaphoreType.DMA((W_SLOTS,)),
        ],
        compiler_params=_cparams(("arbitrary",)),
        name="gla_front",
    )(xp, xs, g, w_in_f32, wa2p, ba)


def _chunk_cumsum(la_ref, r0):
    sub = lax.broadcasted_iota(jnp.int32, (SUBLANES, GLA_DK), 0)
    groups = []
    carry = None
    for gi in range(CHUNK // SUBLANES):
        x = la_ref[pl.ds(r0 + gi * SUBLANES, SUBLANES), :]
        for s in (1, 2, 4):
            x = x + jnp.where(sub >= s, pltpu.roll(x, s, 0), 0.0)
        if carry is not None:
            x = x + carry
        carry = x[SUBLANES - 1:SUBLANES, :]
        groups.append(x)
    return jnp.concatenate(groups, axis=0)


def _gla_core_kernel(*refs, n_chunks, has_state):
    if has_state:
        q_ref, k_ref, v_ref, r_ref, la_ref, gn_ref, s0_ref, og_ref, so_ref, s_ref = refs
    else:
        q_ref, k_ref, v_ref, r_ref, la_ref, gn_ref, og_ref, so_ref, s_ref = refs
    t = pl.program_id(1)

    @pl.when(t == 0)
    def _():
        if has_state:
            s_ref[...] = s0_ref[0]
        else:
            s_ref[...] = jnp.zeros_like(s_ref)

    row = lax.broadcasted_iota(jnp.int32, (CHUNK, CHUNK), 0)
    col = lax.broadcasted_iota(jnp.int32, (CHUNK, CHUNK), 1)
    causal = row >= col

    def chunk_body(c, carry):
        r0 = pl.multiple_of(c * CHUNK, CHUNK)
        rows = pl.ds(r0, CHUNK)
        b2 = _chunk_cumsum(la_ref, r0) * LOG2E
        bl2 = b2[CHUNK - 1:CHUNK, :]
        q = q_ref[rows, :].astype(F32)
        k = k_ref[rows, :].astype(F32)
        qe = (q * (jnp.exp2(b2) * (DK_HEAD ** -0.5))).astype(BF16)
        ke = (k * jnp.exp2(-b2)).astype(BF16)
        kd_t = jnp.transpose(k * jnp.exp2(bl2 - b2)).astype(BF16)
        dec_t = jnp.transpose(jnp.broadcast_to(jnp.exp2(bl2), (SUBLANES, GLA_DK)))
        atts = []
        for h in range(GLA_HEADS):
            ks = slice(h * DK_HEAD, (h + 1) * DK_HEAD)
            att = lax.dot_general(qe[:, ks], ke[:, ks], (((1,), (1,)), ((), ())), preferred_element_type=F32)
            atts.append(jnp.where(causal, att, 0.0).astype(BF16))
        for h in range(GLA_HEADS):
            ks = slice(h * DK_HEAD, (h + 1) * DK_HEAD)
            vs = slice(h * DV_HEAD, (h + 1) * DV_HEAD)
            qh = qe[:, ks]
            vh = v_ref[rows, vs]
            s_old = s_ref[h]
            o = jnp.dot(qh, s_old.astype(BF16), preferred_element_type=F32) + jnp.dot(
                atts[h], vh, preferred_element_type=F32)
            s_ref[h] = dec_t[ks, 0:1] * s_old + jnp.dot(kd_t[ks, :], vh, preferred_element_type=F32)
            on = _rmsnorm_rows(o, gn_ref[:, vs])
            gate = _silu(r_ref[rows, vs].astype(F32))
            og_ref[rows, vs] = (on * gate).astype(BF16)
        return carry

    lax.fori_loop(0, n_chunks, chunk_body, 0, unroll=4 if n_chunks % 4 == 0 else 1)

    @pl.when(t == pl.num_programs(1) - 1)
    def _():
        so_ref[0] = s_ref[...]


def _gla_core(qkvr, loga, gnorm, s0, n_seq, seq_len, row0, tt):
    has_state = s0 is not None
    tps = seq_len // tt
    blk0 = row0 // tt

    def rows(width, col_blk):
        return pl.BlockSpec((tt, width), lambda b, t: (blk0 + b * tps + t, col_blk))

    in_specs = [rows(GLA_DK, 0), rows(GLA_DK, 1), rows(GLA_DV, 1), rows(GLA_DV, 2), rows(GLA_DK, 0),
                pl.BlockSpec((1, GLA_DV), lambda b, t: (0, 0))]
    args = [qkvr, qkvr, qkvr, qkvr, loga, gnorm]
    state_spec = pl.BlockSpec((1, GLA_HEADS, DK_HEAD, DV_HEAD), lambda b, t: (b, 0, 0, 0))
    if has_state:
        in_specs.append(state_spec)
        args.append(s0)
    return pl.pallas_call(
        functools.partial(_gla_core_kernel, n_chunks=tt // CHUNK, has_state=has_state),
        grid=(n_seq, tps),
        in_specs=in_specs,
        out_specs=[pl.BlockSpec((tt, GLA_DV), lambda b, t: (b * tps + t, 0)), state_spec],
        out_shape=[
            jax.ShapeDtypeStruct((n_seq * seq_len, GLA_DV), BF16),
            jax.ShapeDtypeStruct((n_seq, GLA_HEADS, DK_HEAD, DV_HEAD), F32),
        ],
        scratch_shapes=[pltpu.VMEM((GLA_HEADS, DK_HEAD, DV_HEAD), F32)],
        compiler_params=_cparams(("arbitrary", "arbitrary")),
        name="gla_core",
    )(*args)


def _outproj0_kernel(ap_ref, as_ref, w_hbm, xp_ref, xs_ref, o_ref, w_ref, stage, sem, *, n_p, tn):
    @pl.when(pl.program_id(0) == 0)
    def _():
        _stream_rows_bf16(w_hbm, w_ref, stage, sem, W_ROWS_OUT)

    is_p = pl.program_id(0) < n_p
    a = jnp.where(is_p, ap_ref[...], as_ref[...])
    for n0 in range(0, D_MODEL, tn):
        cols = slice(n0, n0 + tn)
        x = jnp.where(is_p, xp_ref[:, cols], xs_ref[:, cols])
        o_ref[:, cols] = x + jnp.dot(a, w_ref[:, cols], preferred_element_type=F32)


def _outproj0(og_p, og_s, w, xp, xs, tn=512):
    n_p, n_s = xp.shape[0] // TM_OUT, xs.shape[0] // TM_OUT
    n = n_p + n_s
    rows = functools.partial(_tile_spec, tm=TM_OUT)
    return pl.pallas_call(
        functools.partial(_outproj0_kernel, n_p=n_p, tn=tn),
        grid=(n,),
        in_specs=[
            rows(GLA_DV, 0, 0, n_p),
            rows(GLA_DV, 0, n_p, n_s),
            pl.BlockSpec(memory_space=pl.ANY),
            rows(D_MODEL, 0, 0, n_p),
            rows(D_MODEL, 0, n_p, n_s),
        ],
        out_specs=rows(D_MODEL, 0, 0, n),
        out_shape=jax.ShapeDtypeStruct((n * TM_OUT, D_MODEL), F32),
        scratch_shapes=[
            pltpu.VMEM((GLA_DV, D_MODEL), BF16),
            pltpu.VMEM((W_SLOTS, W_ROWS_OUT, D_MODEL), F32),
            pltpu.SemaphoreType.DMA((W_SLOTS,)),
        ],
        compiler_params=_cparams(("arbitrary",)),
        name="outproj0",
    )(og_p, og_s, w, xp, xs)


def _time_rows(start, n):
    return pl.ds(ROW_STRIDE * start, n, stride=ROW_STRIDE)


def _conv_front_kernel(x_ref, g_ref, w_hbm, b_ref, cw_ref, cb_ref, st_ref, sz_ref, c_ref, sop_ref, sos_ref,
                       h_cur, h_nxt, u_stage, buf, hist, w_ref, stage, sem, *, n_p, tiles_per_seq, tn):
    s = pl.program_id(0)
    nseg = TM // SEG
    region = HALO + SEG
    n_slab = CONV_DIM // LANES
    first = HALO - CONV_PAD

    @pl.when(s == 0)
    def _():
        h_nxt[...] = jnp.zeros_like(h_nxt)
        u_stage[...] = jnp.zeros_like(u_stage)
        hist[...] = jnp.zeros_like(hist)
        _stream_rows_bf16(w_hbm, w_ref, stage, sem, W_ROWS_IN)

    h_cur[...] = h_nxt[...]
    t_conv = jnp.maximum(s - 2, 0)
    is_p = t_conv < n_p
    for g in range(nseg):
        for sl in range(n_slab):
            lanes = slice(sl * LANES, (sl + 1) * LANES)
            buf[sl, _time_rows(g * region + HALO, SEG), :] = u_stage[g * SEG:(g + 1) * SEG, lanes]

    @pl.when(is_p)
    def _():
        @pl.when(lax.rem(t_conv, tiles_per_seq) == 0)
        def _():
            hist[...] = jnp.zeros_like(hist)

        for sl in range(n_slab):
            lanes = slice(sl * LANES, (sl + 1) * LANES)
            buf[sl, _time_rows(0, HALO), :] = hist[:, lanes]
            for g in range(1, nseg):
                buf[sl, _time_rows(g * region, HALO), :] = u_stage[g * SEG - HALO:g * SEG, lanes]
        tail = u_stage[TM - HALO:TM, :]
        hist[...] = tail

        @pl.when(lax.rem(t_conv, tiles_per_seq) == tiles_per_seq - 1)
        def _():
            sop_ref[t_conv // tiles_per_seq] = tail

    @pl.when(jnp.logical_not(is_p))
    def _():
        for g in range(nseg):
            seq_i = (t_conv - n_p) * nseg + g
            for sl in range(n_slab):
                lanes = slice(sl * LANES, (sl + 1) * LANES)
                buf[sl, _time_rows(g * region, HALO), :] = st_ref[seq_i, :, lanes]
            sos_ref[seq_i] = u_stage[(g + 1) * SEG - HALO:(g + 1) * SEG, :]

    n_chunks = CONV_DIM // tn
    slabs_per_chunk = n_slab // n_chunks
    prev_acc = None
    for j in range(n_chunks):
        n0 = j * tn
        cols = slice(n0, n0 + tn)
        gcols = slice(CONV_DIM + n0, CONV_DIM + n0 + tn)
        zcols = slice(2 * CONV_DIM + n0, 2 * CONV_DIM + n0 + tn)
        a = jnp.dot(h_cur[...], w_ref[:, cols], preferred_element_type=F32) + b_ref[:, cols]
        ga = jnp.dot(h_cur[...], w_ref[:, gcols], preferred_element_type=F32) + b_ref[:, gcols]
        u_stage[:, cols] = a * jax.nn.sigmoid(ga)
        z = jnp.dot(h_cur[...], w_ref[:, zcols], preferred_element_type=F32) + b_ref[:, zcols]
        sz_ref[:, cols] = _silu(z).astype(BF16)

        for sl in range(j * slabs_per_chunk, (j + 1) * slabs_per_chunk):
            lanes = slice(sl * LANES, (sl + 1) * LANES)
            for g in range(nseg):
                acc = jnp.broadcast_to(cb_ref[:, lanes], (SEG, LANES))
                if prev_acc is not None:
                    acc = _after(acc, prev_acc)
                for kk in range(CONV_WIDTH):
                    acc = acc + buf[sl, _time_rows(g * region + first + kk, SEG), :] * cw_ref[kk:kk + 1, lanes]
                c_ref[g * SEG:(g + 1) * SEG, lanes] = acc
                prev_acc = acc

    h_nxt[...] = _rmsnorm_rows(x_ref[...], g_ref[...]).astype(BF16)


def _conv_front(x1, g, w, b, cw, cb, st, n_p, n_prompt_seq, tn=1024):
    n = x1.shape[0] // TM
    n_stream = st.shape[0]
    return pl.pallas_call(
        functools.partial(_conv_front_kernel, n_p=n_p, tiles_per_seq=n_p // n_prompt_seq, tn=tn),
        grid=(n + 2,),
        in_specs=[
            _tile_spec(D_MODEL, 0, 0, n),
            _resident((1, D_MODEL)),
            pl.BlockSpec(memory_space=pl.ANY),
            _resident((1, 3 * CONV_DIM)),
            _resident((CONV_WIDTH, CONV_DIM)),
            _resident((1, CONV_DIM)),
            _resident((n_stream, HALO, CONV_DIM)),
        ],
        out_specs=[
            _tile_spec(CONV_DIM, 1, 0, n),
            _tile_spec(CONV_DIM, 2, 0, n),
            pl.BlockSpec((n_prompt_seq, HALO, CONV_DIM), lambda s: (0, 0, 0)),
            pl.BlockSpec((n_stream, HALO, CONV_DIM), lambda s: (0, 0, 0)),
        ],
        out_shape=[
            jax.ShapeDtypeStruct((n * TM, CONV_DIM), BF16),
            jax.ShapeDtypeStruct((n * TM, CONV_DIM), F32),
            jax.ShapeDtypeStruct((n_prompt_seq, HALO, CONV_DIM), F32),
            jax.ShapeDtypeStruct((n_stream, HALO, CONV_DIM), F32),
        ],
        scratch_shapes=[
            pltpu.VMEM((TM, D_MODEL), BF16),
            pltpu.VMEM((TM, D_MODEL), BF16),
            pltpu.VMEM((TM, CONV_DIM), F32),
            pltpu.VMEM((CONV_DIM // LANES, ROW_STRIDE * (TM // SEG) * (HALO + SEG), LANES), F32),
            pltpu.VMEM((HALO, CONV_DIM), F32),
            pltpu.VMEM((D_MODEL, 3 * CONV_DIM), BF16),
            pltpu.VMEM((W_SLOTS, W_ROWS_IN, 3 * CONV_DIM), F32),
            pltpu.SemaphoreType.DMA((W_SLOTS,)),
        ],
        compiler_params=_cparams(("arbitrary",)),
        name="conv_front",
    )(x1, g, w, b, cw, cb, st)


def _outproj1_kernel(c_ref, sz_ref, lg_ref, lb_ref, w_hbm, x_ref, fg_ref, yp_ref, ys_ref, a_scr, y_scr,
                     w_ref, stage, sem, *, n_p, tn):
    @pl.when(pl.program_id(0) == 0)
    def _():
        _stream_rows_bf16(w_hbm, w_ref, stage, sem, W_ROWS_OUT1)

    c = c_ref[...]
    mu = jnp.mean(c, axis=-1, keepdims=True)
    xc = c - mu
    ln = xc * lax.rsqrt(jnp.mean(xc * xc, axis=-1, keepdims=True) + EPS) * lg_ref[...] + lb_ref[...]
    a_scr[...] = (_silu(ln) * sz_ref[...].astype(F32)).astype(BF16)

    for n0 in range(0, D_MODEL, tn):
        cols = slice(n0, n0 + tn)
        y_scr[:, cols] = x_ref[:, cols] + jnp.dot(a_scr[...], w_ref[:, cols], preferred_element_type=F32)
    y = y_scr[...]
    scale = lax.rsqrt(jnp.mean(y * y, axis=-1, keepdims=True) + EPS)
    is_p = pl.program_id(0) < n_p

    @pl.when(is_p)
    def _():
        yp_ref[...] = y_scr[...] * scale * fg_ref[...]

    @pl.when(jnp.logical_not(is_p))
    def _():
        ys_ref[...] = y_scr[...] * scale * fg_ref[...]


def _outproj1(c, sz, lg, lb, w, x1, fg, n_p_rows, tn=512):
    n = x1.shape[0] // TM_OUT
    n_p = n_p_rows // TM_OUT
    n_s = n - n_p
    vec = _resident((1, D_MODEL))
    rows = functools.partial(_tile_spec, tm=TM_OUT)
    return pl.pallas_call(
        functools.partial(_outproj1_kernel, n_p=n_p, tn=tn),
        grid=(n,),
        in_specs=[
            rows(CONV_DIM, 0, 0, n),
            rows(CONV_DIM, 0, 0, n),
            vec, vec,
            pl.BlockSpec(memory_space=pl.ANY),
            rows(D_MODEL, 0, 0, n),
            vec,
        ],
        out_specs=[rows(D_MODEL, 0, 0, n_p), rows(D_MODEL, 0, n_p, n_s)],
        out_shape=[
            jax.ShapeDtypeStruct((n_p * TM_OUT, D_MODEL), F32),
            jax.ShapeDtypeStruct((n_s * TM_OUT, D_MODEL), F32),
        ],
        scratch_shapes=[
            pltpu.VMEM((TM_OUT, CONV_DIM), BF16),
            pltpu.VMEM((TM_OUT, D_MODEL), F32),
            pltpu.VMEM((CONV_DIM, D_MODEL), BF16),
            pltpu.VMEM((W_SLOTS, W_ROWS_OUT1, D_MODEL), F32),
            pltpu.SemaphoreType.DMA((W_SLOTS,)),
        ],
        compiler_params=_cparams(("arbitrary",)),
        name="outproj1",
    )(c, sz, lg, lb, w, x1, fg)


def _forward(x_prompt, x_sample, gla_state, conv_hist, p, tt_prompt=1024):
    n_pseq, p_len, _ = x_prompt.shape
    n_sseq, s_len, _ = x_sample.shape
    assert s_len == SEG and p_len % TM == 0 and (n_sseq * s_len) % TM == 0
    xp = x_prompt.reshape(n_pseq * p_len, D_MODEL)
    xs = x_sample.reshape(n_sseq * s_len, D_MODEL)
    n_p = xp.shape[0] // TM

    qkvr, loga = _gla_front(xp, xs, p["norm_g0"], p["gla_w_in_t"], p["gla_wa2"], p["gla_ba"])
    og_p, gla_p = _gla_core(qkvr, loga, p["gla_gn"], None, n_pseq, p_len, 0, tt_prompt)
    og_s, gla_s = _gla_core(qkvr, loga, p["gla_gn"], gla_state, n_sseq, s_len, xp.shape[0], s_len)
    x1 = _outproj0(og_p, og_s, p["gla_w_out"], xp, xs)

    sz, c, conv_p, conv_s = _conv_front(x1, p["norm_g1"], p["conv_w_in"], p["conv_b_in"], p["conv_w"],
                                        p["conv_b"], conv_hist, n_p, n_pseq)
    y_p, y_s = _outproj1(c, sz, p["conv_ln_g"], p["conv_ln_b"], p["conv_w_out"], x1, p["final_g"], xp.shape[0])
    keep = slice(HALO - CONV_PAD, HALO)
    return (y_p.reshape(x_prompt.shape), y_s.reshape(x_sample.shape), gla_p[None], conv_p[None, :, keep, :],
            gla_s[None], conv_s[None, :, keep, :])


def _prepare(norm_g, final_norm_g, gla_w_in, gla_w_a2, gla_b_a, gla_norm_g, gla_w_out, conv_w_in, conv_b_in,
             conv_w, conv_b, conv_ln_g, conv_ln_b, conv_w_out):
    w_in0 = gla_w_in[0]
    return {
        "norm_g0": norm_g[0:1],
        "norm_g1": norm_g[1:2],
        "final_g": final_norm_g[None, :],
        "gla_w_in_t": jnp.swapaxes(w_in0, 0, 1),
        "gla_wa2": jnp.pad(gla_w_a2[0], ((0, LANES - GATE_RANK), (0, 0))).astype(BF16),
        "gla_ba": gla_b_a[0:1],
        "gla_gn": gla_norm_g[0:1],
        "gla_w_out": gla_w_out[0],
        "conv_w_in": conv_w_in[0],
        "conv_b_in": conv_b_in[0:1],
        "conv_w": conv_w[0],
        "conv_b": conv_b[0:1],
        "conv_ln_g": conv_ln_g[0:1],
        "conv_ln_b": conv_ln_b[0:1],
        "conv_w_out": conv_w_out[0],
    }


def kernel(x_prompt, x_sample, state_gla, state_conv, norm_g, final_norm_g, gla_w_in, gla_w_a2, gla_b_a,
           gla_norm_g, gla_w_out, conv_w_in, conv_b_in, conv_w, conv_b, conv_ln_g, conv_ln_b, conv_w_out):
    p = _prepare(norm_g, final_norm_g, gla_w_in, gla_w_a2, gla_b_a, gla_norm_g, gla_w_out, conv_w_in, conv_b_in,
                 conv_w, conv_b, conv_ln_g, conv_ln_b, conv_w_out)
    conv_hist = jnp.pad(state_conv[0], ((0, 0), (HALO - CONV_PAD, 0), (0, 0)))
    return _forward(x_prompt, x_sample, state_gla[0], conv_hist, p)
```

```python
import functools

import jax
import jax.numpy as jnp
from jax import lax
from jax.experimental import pallas as pl
from jax.experimental.pallas import tpu as pltpu

F32 = jnp.float32
BF16 = jnp.bfloat16

D_MODEL = 2048
CHUNK = 64
GLA_HEADS = 4
GLA_DK = 1024
GLA_DV = 2048
DK_HEAD = GLA_DK // GLA_HEADS
DV_HEAD = GLA_DV // GLA_HEADS
GATE_RANK = 16
GATE_NORM = 16.0
GLA_MAIN = 2 * GLA_DK + 2 * GLA_DV
CONV_DIM = 2048
CONV_WIDTH = 31
CONV_PAD = CONV_WIDTH - 1
EPS = 1e-6
LOG2E = 1.4426950408889634

SUBLANES = 8
LANES = 128
HALO = 32
ROW_STRIDE = 2
TM = 256
TM_OUT = 512
W_CHUNK = 256
W_SLOTS = 4
W_ROWS_IN = 32
W_ROWS_OUT = 128
W_ROWS_OUT1 = 64
SEG = 64
VMEM_LIMIT = 56 * 1024 * 1024


def _cparams(sem):
    return pltpu.CompilerParams(dimension_semantics=sem, vmem_limit_bytes=VMEM_LIMIT)


def _after(x, dep):
    half = jnp.uint32(16)
    zero = lax.shift_right_logical(lax.shift_right_logical(lax.bitcast_convert_type(dep, jnp.uint32), half), half)
    return lax.bitcast_convert_type(lax.bitcast_convert_type(x, jnp.uint32) | zero, F32)


def _silu(x):
    return x * jax.nn.sigmoid(x)


def _rmsnorm_rows(x, g):
    ms = jnp.mean(x * x, axis=-1, keepdims=True)
    return x * lax.rsqrt(ms + EPS) * g


def _stream_rows_bf16(w_hbm, w_scr, stage, sem, rows):
    n_slots = stage.shape[0]
    n_w = w_scr.shape[0] // rows

    def chunk(j):
        slot = j % n_slots
        return pltpu.make_async_copy(w_hbm.at[pl.ds(j * rows, rows), :], stage.at[slot], sem.at[slot])

    for j in range(n_slots - 1):
        chunk(j).start(priority=j % 2)
    for j in range(n_w):
        nxt = j + n_slots - 1
        if nxt < n_w:
            chunk(nxt).start(priority=nxt % 2)
        chunk(j).wait()
        w_scr[j * rows:(j + 1) * rows, :] = stage[j % n_slots].astype(BF16)


def _resident(shape):
    return pl.BlockSpec(shape, lambda *_: (0,) * len(shape), pipeline_mode=pl.Buffered(1))


def _tile_spec(width, lag, first, count, tm=TM):
    return pl.BlockSpec((tm, width), lambda s: (jnp.clip(s - lag - first, 0, count - 1), 0))


def _gla_front_kernel(xp_ref, xs_ref, g_ref, w_hbm, wa2_ref, ba_ref, o_ref, la_ref, h_cur, h_nxt,
                      w_ref, wa1t_ref, stage, sem, *, n_p, tn):
    s = pl.program_id(0)

    def weight_chunk(j):
        slot = j % W_SLOTS
        return pltpu.make_async_copy(w_hbm.at[pl.ds(j * W_CHUNK, W_CHUNK), :], stage.at[slot], sem.at[slot])

    @pl.when(s == 0)
    def _():
        h_nxt[...] = jnp.zeros_like(h_nxt)
        n_w = GLA_MAIN // W_CHUNK
        for j in range(W_SLOTS - 1):
            weight_chunk(j).start(priority=j % 2)
        for j in range(n_w):
            nxt = j + W_SLOTS - 1
            if nxt < n_w:
                weight_chunk(nxt).start(priority=nxt % 2)
            weight_chunk(j).wait()
            w_ref[:, j * W_CHUNK:(j + 1) * W_CHUNK] = jnp.transpose(stage[j % W_SLOTS]).astype(BF16)
        gate_rows = pltpu.make_async_copy(w_hbm.at[pl.ds(GLA_MAIN, GATE_RANK), :],
                                          stage.at[0, pl.ds(0, GATE_RANK), :], sem.at[0])
        gate_rows.start()
        gate_rows.wait()
        wa1t_ref[...] = jnp.zeros_like(wa1t_ref)
        wa1t_ref[0:GATE_RANK, :] = stage[0, 0:GATE_RANK, :].astype(BF16)

    h_cur[...] = h_nxt[...]

    h = h_cur[...]
    a1 = lax.dot_general(h, wa1t_ref[...], (((1,), (1,)), ((), ())), preferred_element_type=F32)
    z = jnp.dot(a1.astype(BF16), wa2_ref[...], preferred_element_type=F32) + ba_ref[...]
    la_ref[...] = (jnp.minimum(z, 0.0) - jnp.log(1.0 + jnp.exp(-jnp.abs(z)))) * (1.0 / GATE_NORM)
    for n0 in range(0, GLA_MAIN, tn):
        cols = slice(n0, n0 + tn)
        o_ref[:, cols] = jnp.dot(h_cur[...], w_ref[:, cols], preferred_element_type=F32).astype(BF16)

    x = jnp.where(s < n_p, xp_ref[...], xs_ref[...])
    h_nxt[...] = _rmsnorm_rows(x, g_ref[...]).astype(BF16)


def _gla_front(xp, xs, g, w_in_f32, wa2p, ba, tn=512):
    n_p, n_s = xp.shape[0] // TM, xs.shape[0] // TM
    n = n_p + n_s
    return pl.pallas_call(
        functools.partial(_gla_front_kernel, n_p=n_p, tn=tn),
        grid=(n + 1,),
        in_specs=[
            _tile_spec(D_MODEL, 0, 0, n_p),
            _tile_spec(D_MODEL, 0, n_p, n_s),
            _resident((1, D_MODEL)),
            pl.BlockSpec(memory_space=pl.ANY),
            _resident((LANES, GLA_DK)),
            _resident((1, GLA_DK)),
        ],
        out_specs=[_tile_spec(GLA_MAIN, 1, 0, n), _tile_spec(GLA_DK, 1, 0, n)],
        out_shape=[
            jax.ShapeDtypeStruct((n * TM, GLA_MAIN), BF16),
            jax.ShapeDtypeStruct((n * TM, GLA_DK), F32),
        ],
        scratch_shapes=[
            pltpu.VMEM((TM, D_MODEL), BF16),
            pltpu.VMEM((TM, D_MODEL), BF16),
            pltpu.VMEM((D_MODEL, GLA_MAIN), BF16),
            pltpu.VMEM((LANES, D_MODEL), BF16),
            pltpu.VMEM((W_SLOTS, W_CHUNK, D_MODEL), F32),
            pltpu.SemaphoreType.DMA((W_SLOTS,)),
        ],
        compiler_params=_cparams(("arbitrary",)),
        name="gla_front",
    )(xp, xs, g, w_in_f32, wa2p, ba)


def _chunk_cumsum(la_ref, r0):
    sub = lax.broadcasted_iota(jnp.int32, (SUBLANES, GLA_DK), 0)
    groups = []
    carry = None
    for gi in range(CHUNK // SUBLANES):
        x = la_ref[pl.ds(r0 + gi * SUBLANES, SUBLANES), :]
        for s in (1, 2, 4):
            x = x + jnp.where(sub >= s, pltpu.roll(x, s, 0), 0.0)
        if carry is not None:
            x = x + carry
        carry = x[SUBLANES - 1:SUBLANES, :]
        groups.append(x)
    return jnp.concatenate(groups, axis=0)


def _gla_core_kernel(*refs, n_chunks, has_state):
    if has_state:
        q_ref, k_ref, v_ref, r_ref, la_ref, gn_ref, s0_ref, og_ref, so_ref, s_ref = refs
    else:
        q_ref, k_ref, v_ref, r_ref, la_ref, gn_ref, og_ref, so_ref, s_ref = refs
    t = pl.program_id(1)

    @pl.when(t == 0)
    def _():
        if has_state:
            s_ref[...] = s0_ref[0]
        else:
            s_ref[...] = jnp.zeros_like(s_ref)

    row = lax.broadcasted_iota(jnp.int32, (CHUNK, CHUNK), 0)
    col = lax.broadcasted_iota(jnp.int32, (CHUNK, CHUNK), 1)
    causal = row >= col

    def chunk_body(c, carry):
        r0 = pl.multiple_of(c * CHUNK, CHUNK)
        rows = pl.ds(r0, CHUNK)
        b2 = _chunk_cumsum(la_ref, r0) * LOG2E
        bl2 = b2[CHUNK - 1:CHUNK, :]
        q = q_ref[rows, :].astype(F32)
        k = k_ref[rows, :].astype(F32)
        qe = (q * (jnp.exp2(b2) * (DK_HEAD ** -0.5))).astype(BF16)
        ke = (k * jnp.exp2(-b2)).astype(BF16)
        kd_t = jnp.transpose(k * jnp.exp2(bl2 - b2)).astype(BF16)
        dec_t = jnp.transpose(jnp.broadcast_to(jnp.exp2(bl2), (SUBLANES, GLA_DK)))
        atts = []
        for h in range(GLA_HEADS):
            ks = slice(h * DK_HEAD, (h + 1) * DK_HEAD)
            att = lax.dot_general(qe[:, ks], ke[:, ks], (((1,), (1,)), ((), ())), preferred_element_type=F32)
            atts.append(jnp.where(causal, att, 0.0).astype(BF16))
        for h in range(GLA_HEADS):
            ks = slice(h * DK_HEAD, (h + 1) * DK_HEAD)
            vs = slice(h * DV_HEAD, (h + 1) * DV_HEAD)
            qh = qe[:, ks]
            vh = v_ref[rows, vs]
            s_old = s_ref[h]
            o = jnp.dot(qh, s_old.astype(BF16), preferred_element_type=F32) + jnp.dot(
                atts[h], vh, preferred_element_type=F32)
            s_ref[h] = dec_t[ks, 0:1] * s_old + jnp.dot(kd_t[ks, :], vh, preferred_element_type=F32)
            on = _rmsnorm_rows(o, gn_ref[:, vs])
            gate = _silu(r_ref[rows, vs].astype(F32))
            og_ref[rows, vs] = (on * gate).astype(BF16)
        return carry

    lax.fori_loop(0, n_chunks, chunk_body, 0, unroll=4 if n_chunks % 4 == 0 else 1)

    @pl.when(t == pl.num_programs(1) - 1)
    def _():
        so_ref[0] = s_ref[...]


def _gla_core(qkvr, loga, gnorm, s0, n_seq, seq_len, row0, tt):
    has_state = s0 is not None
    tps = seq_len // tt
    blk0 = row0 // tt

    def rows(width, col_blk):
        return pl.BlockSpec((tt, width), lambda b, t: (blk0 + b * tps + t, col_blk))

    in_specs = [rows(GLA_DK, 0), rows(GLA_DK, 1), rows(GLA_DV, 1), rows(GLA_DV, 2), rows(GLA_DK, 0),
                pl.BlockSpec((1, GLA_DV), lambda b, t: (0, 0))]
    args = [qkvr, qkvr, qkvr, qkvr, loga, gnorm]
    state_spec = pl.BlockSpec((1, GLA_HEADS, DK_HEAD, DV_HEAD), lambda b, t: (b, 0, 0, 0))
    if has_state:
        in_specs.append(state_spec)
        args.append(s0)
    return pl.pallas_call(
        functools.partial(_gla_core_kernel, n_chunks=tt // CHUNK, has_state=has_state),
        grid=(n_seq, tps),
        in_specs=in_specs,
        out_specs=[pl.BlockSpec((tt, GLA_DV), lambda b, t: (b * tps + t, 0)), state_spec],
        out_shape=[
            jax.ShapeDtypeStruct((n_seq * seq_len, GLA_DV), BF16),
            jax.ShapeDtypeStruct((n_seq, GLA_HEADS, DK_HEAD, DV_HEAD), F32),
        ],
        scratch_shapes=[pltpu.VMEM((GLA_HEADS, DK_HEAD, DV_HEAD), F32)],
        compiler_params=_cparams(("arbitrary", "arbitrary")),
        name="gla_core",
    )(*args)


def _outproj0_kernel(ap_ref, as_ref, w_hbm, xp_ref, xs_ref, o_ref, w_ref, stage, sem, *, n_p, tn):
    @pl.when(pl.program_id(0) == 0)
    def _():
        _stream_rows_bf16(w_hbm, w_ref, stage, sem, W_ROWS_OUT)

    is_p = pl.program_id(0) < n_p
    a = jnp.where(is_p, ap_ref[...], as_ref[...])
    for n0 in range(0, D_MODEL, tn):
        cols = slice(n0, n0 + tn)
        x = jnp.where(is_p, xp_ref[:, cols], xs_ref[:, cols])
        o_ref[:, cols] = x + jnp.dot(a, w_ref[:, cols], preferred_element_type=F32)


def _outproj0(og_p, og_s, w, xp, xs, tn=512):
    n_p, n_s = xp.shape[0] // TM_OUT, xs.shape[0] // TM_OUT
    n = n_p + n_s
    rows = functools.partial(_tile_spec, tm=TM_OUT)
    return pl.pallas_call(
        functools.partial(_outproj0_kernel, n_p=n_p, tn=tn),
        grid=(n,),
        in_specs=[
            rows(GLA_DV, 0, 0, n_p),
            rows(GLA_DV, 0, n_p, n_s),
            pl.BlockSpec(memory_space=pl.ANY),
            rows(D_MODEL, 0, 0, n_p),
            rows(D_MODEL, 0, n_p, n_s),
        ],
        out_specs=rows(D_MODEL, 0, 0, n),
        out_shape=jax.ShapeDtypeStruct((n * TM_OUT, D_MODEL), F32),
        scratch_shapes=[
            pltpu.VMEM((GLA_DV, D_MODEL), BF16),
            pltpu.VMEM((W_SLOTS, W_ROWS_OUT, D_MODEL), F32),
            pltpu.SemaphoreType.DMA((W_SLOTS,)),
        ],
        compiler_params=_cparams(("arbitrary",)),
        name="outproj0",
    )(og_p, og_s, w, xp, xs)


def _time_rows(start, n):
    return pl.ds(ROW_STRIDE * start, n, stride=ROW_STRIDE)


def _conv_front_kernel(x_ref, g_ref, w_hbm, b_ref, cw_ref, cb_ref, st_ref, sz_ref, c_ref, sop_ref, sos_ref,
                       h_cur, h_nxt, u_stage, buf, hist, w_ref, stage, sem, *, n_p, tiles_per_seq, tn):
    s = pl.program_id(0)
    nseg = TM // SEG
    region = HALO + SEG
    n_slab = CONV_DIM // LANES
    first = HALO - CONV_PAD

    @pl.when(s == 0)
    def _():
        h_nxt[...] = jnp.zeros_like(h_nxt)
        u_stage[...] = jnp.zeros_like(u_stage)
        hist[...] = jnp.zeros_like(hist)
        _stream_rows_bf16(w_hbm, w_ref, stage, sem, W_ROWS_IN)

    h_cur[...] = h_nxt[...]
    t_conv = jnp.maximum(s - 2, 0)
    is_p = t_conv < n_p
    for g in range(nseg):
        for sl in range(n_slab):
            lanes = slice(sl * LANES, (sl + 1) * LANES)
            buf[sl, _time_rows(g * region + HALO, SEG), :] = u_stage[g * SEG:(g + 1) * SEG, lanes]

    @pl.when(is_p)
    def _():
        @pl.when(lax.rem(t_conv, tiles_per_seq) == 0)
        def _():
            hist[...] = jnp.zeros_like(hist)

        for sl in range(n_slab):
            lanes = slice(sl * LANES, (sl + 1) * LANES)
            buf[sl, _time_rows(0, HALO), :] = hist[:, lanes]
            for g in range(1, nseg):
                buf[sl, _time_rows(g * region, HALO), :] = u_stage[g * SEG - HALO:g * SEG, lanes]
        tail = u_stage[TM - HALO:TM, :]
        hist[...] = tail

        @pl.when(lax.rem(t_conv, tiles_per_seq) == tiles_per_seq - 1)
        def _():
            sop_ref[t_conv // tiles_per_seq] = tail

    @pl.when(jnp.logical_not(is_p))
    def _():
        for g in range(nseg):
            seq_i = (t_conv - n_p) * nseg + g
            for sl in range(n_slab):
                lanes = slice(sl * LANES, (sl + 1) * LANES)
                buf[sl, _time_rows(g * region, HALO), :] = st_ref[seq_i, :, lanes]
            sos_ref[seq_i] = u_stage[(g + 1) * SEG - HALO:(g + 1) * SEG, :]

    n_chunks = CONV_DIM // tn
    slabs_per_chunk = n_slab // n_chunks
    prev_acc = None
    for j in range(n_chunks):
        n0 = j * tn
        cols = slice(n0, n0 + tn)
        gcols = slice(CONV_DIM + n0, CONV_DIM + n0 + tn)
        zcols = slice(2 * CONV_DIM + n0, 2 * CONV_DIM + n0 + tn)
        a = jnp.dot(h_cur[...], w_ref[:, cols], preferred_element_type=F32) + b_ref[:, cols]
        ga = jnp.dot(h_cur[...], w_ref[:, gcols], preferred_element_type=F32) + b_ref[:, gcols]
        u_stage[:, cols] = a * jax.nn.sigmoid(ga)
        z = jnp.dot(h_cur[...], w_ref[:, zcols], preferred_element_type=F32) + b_ref[:, zcols]
        sz_ref[:, cols] = _silu(z).astype(BF16)

        for sl in range(j * slabs_per_chunk, (j + 1) * slabs_per_chunk):
            lanes = slice(sl * LANES, (sl + 1) * LANES)
            for g in range(nseg):
                acc = jnp.broadcast_to(cb_ref[:, lanes], (SEG, LANES))
                if prev_acc is not None:
                    acc = _after(acc, prev_acc)
                for kk in range(CONV_WIDTH):
                    acc = acc + buf[sl, _time_rows(g * region + first + kk, SEG), :] * cw_ref[kk:kk + 1, lanes]
                c_ref[g * SEG:(g + 1) * SEG, lanes] = acc
                prev_acc = acc

    h_nxt[...] = _rmsnorm_rows(x_ref[...], g_ref[...]).astype(BF16)


def _conv_front(x1, g, w, b, cw, cb, st, n_p, n_prompt_seq, tn=1024):
    n = x1.shape[0] // TM
    n_stream = st.shape[0]
    return pl.pallas_call(
        functools.partial(_conv_front_kernel, n_p=n_p, tiles_per_seq=n_p // n_prompt_seq, tn=tn),
        grid=(n + 2,),
        in_specs=[
            _tile_spec(D_MODEL, 0, 0, n),
            _resident((1, D_MODEL)),
            pl.BlockSpec(memory_space=pl.ANY),
            _resident((1, 3 * CONV_DIM)),
            _resident((CONV_WIDTH, CONV_DIM)),
            _resident((1, CONV_DIM)),
            _resident((n_stream, HALO, CONV_DIM)),
        ],
        out_specs=[
            _tile_spec(CONV_DIM, 1, 0, n),
            _tile_spec(CONV_DIM, 2, 0, n),
            pl.BlockSpec((n_prompt_seq, HALO, CONV_DIM), lambda s: (0, 0, 0)),
            pl.BlockSpec((n_stream, HALO, CONV_DIM), lambda s: (0, 0, 0)),
        ],
        out_shape=[
            jax.ShapeDtypeStruct((n * TM, CONV_DIM), BF16),
            jax.ShapeDtypeStruct((n * TM, CONV_DIM), F32),
            jax.ShapeDtypeStruct((n_prompt_seq, HALO, CONV_DIM), F32),
            jax.ShapeDtypeStruct((n_stream, HALO, CONV_DIM), F32),
        ],
        scratch_shapes=[
            pltpu.VMEM((TM, D_MODEL), BF16),
            pltpu.VMEM((TM, D_MODEL), BF16),
            pltpu.VMEM((TM, CONV_DIM), F32),
            pltpu.VMEM((CONV_DIM // LANES, ROW_STRIDE * (TM // SEG) * (HALO + SEG), LANES), F32),
            pltpu.VMEM((HALO, CONV_DIM), F32),
            pltpu.VMEM((D_MODEL, 3 * CONV_DIM), BF16),
            pltpu.VMEM((W_SLOTS, W_ROWS_IN, 3 * CONV_DIM), F32),
            pltpu.SemaphoreType.DMA((W_SLOTS,)),
        ],
        compiler_params=_cparams(("arbitrary",)),
        name="conv_front",
    )(x1, g, w, b, cw, cb, st)


def _outproj1_kernel(c_ref, sz_ref, lg_ref, lb_ref, w_hbm, x_ref, fg_ref, yp_ref, ys_ref, a_scr, y_scr,
                     w_ref, stage, sem, *, n_p, tn):
    @pl.when(pl.program_id(0) == 0)
    def _():
        _stream_rows_bf16(w_hbm, w_ref, stage, sem, W_ROWS_OUT1)

    c = c_ref[...]
    mu = jnp.mean(c, axis=-1, keepdims=True)
    xc = c - mu
    ln = xc * lax.rsqrt(jnp.mean(xc * xc, axis=-1, keepdims=True) + EPS) * lg_ref[...] + lb_ref[...]
    a_scr[...] = (_silu(ln) * sz_ref[...].astype(F32)).astype(BF16)

    for n0 in range(0, D_MODEL, tn):
        cols = slice(n0, n0 + tn)
        y_scr[:, cols] = x_ref[:, cols] + jnp.dot(a_scr[...], w_ref[:, cols], preferred_element_type=F32)
    y = y_scr[...]
    scale = lax.rsqrt(jnp.mean(y * y, axis=-1, keepdims=True) + EPS)
    is_p = pl.program_id(0) < n_p

    @pl.when(is_p)
    def _():
        yp_ref[...] = y_scr[...] * scale * fg_ref[...]

    @pl.when(jnp.logical_not(is_p))
    def _():
        ys_ref[...] = y_scr[...] * scale * fg_ref[...]


def _outproj1(c, sz, lg, lb, w, x1, fg, n_p_rows, tn=512):
    n = x1.shape[0] // TM_OUT
    n_p = n_p_rows // TM_OUT
    n_s = n - n_p
    vec = _resident((1, D_MODEL))
    rows = functools.partial(_tile_spec, tm=TM_OUT)
    return pl.pallas_call(
        functools.partial(_outproj1_kernel, n_p=n_p, tn=tn),
        grid=(n,),
        in_specs=[
            rows(CONV_DIM, 0, 0, n),
            rows(CONV_DIM, 0, 0, n),
            vec, vec,
            pl.BlockSpec(memory_space=pl.ANY),
            rows(D_MODEL, 0, 0, n),
            vec,
        ],
        out_specs=[rows(D_MODEL, 0, 0, n_p), rows(D_MODEL, 0, n_p, n_s)],
        out_shape=[
            jax.ShapeDtypeStruct((n_p * TM_OUT, D_MODEL), F32),
            jax.ShapeDtypeStruct((n_s * TM_OUT, D_MODEL), F32),
        ],
        scratch_shapes=[
            pltpu.VMEM((TM_OUT, CONV_DIM), BF16),
            pltpu.VMEM((TM_OUT, D_MODEL), F32),
            pltpu.VMEM((CONV_DIM, D_MODEL), BF16),
            pltpu.VMEM((W_SLOTS, W_ROWS_OUT1, D_MODEL), F32),
            pltpu.SemaphoreType.DMA((W_SLOTS,)),
        ],
        compiler_params=_cparams(("arbitrary",)),
        name="outproj1",
    )(c, sz, lg, lb, w, x1, fg)


def _forward(x_prompt, x_sample, gla_state, conv_hist, p, tt_prompt=1024):
    n_pseq, p_len, _ = x_prompt.shape
    n_sseq, s_len, _ = x_sample.shape
    assert s_len == SEG and p_len % TM == 0 and (n_sseq * s_len) % TM == 0
    xp = x_prompt.reshape(n_pseq * p_len, D_MODEL)
    xs = x_sample.reshape(n_sseq * s_len, D_MODEL)
    n_p = xp.shape[0] // TM

    qkvr, loga = _gla_front(xp, xs, p["norm_g0"], p["gla_w_in_t"], p["gla_wa2"], p["gla_ba"])
    og_p, gla_p = _gla_core(qkvr, loga, p["gla_gn"], None, n_pseq, p_len, 0, tt_prompt)
    og_s, gla_s = _gla_core(qkvr, loga, p["gla_gn"], gla_state, n_sseq, s_len, xp.shape[0], s_len)
    x1 = _outproj0(og_p, og_s, p["gla_w_out"], xp, xs)

    sz, c, conv_p, conv_s = _conv_front(x1, p["norm_g1"], p["conv_w_in"], p["conv_b_in"], p["conv_w"],
                                        p["conv_b"], conv_hist, n_p, n_pseq)
    y_p, y_s = _outproj1(c, sz, p["conv_ln_g"], p["conv_ln_b"], p["conv_w_out"], x1, p["final_g"], xp.shape[0])
    keep = slice(HALO - CONV_PAD, HALO)
    return (y_p.reshape(x_prompt.shape), y_s.reshape(x_sample.shape), gla_p[None], conv_p[None, :, keep, :],
            gla_s[None], conv_s[None, :, keep, :])


def _prepare(norm_g, final_norm_g, gla_w_in, gla_w_a2, gla_b_a, gla_norm_g, gla_w_out, conv_w_in, conv_b_in,
             conv_w, conv_b, conv_ln_g, conv_ln_b, conv_w_out):
    w_in0 = gla_w_in[0]
    return {
        "norm_g0": norm_g[0:1],
        "norm_g1": norm_g[1:2],
        "final_g": final_norm_g[None, :],
        "gla_w_in_t": jnp.swapaxes(w_in0, 0, 1),
        "gla_wa2": jnp.pad(gla_w_a2[0], ((0, LANES - GATE_RANK), (0, 0))).astype(BF16),
        "gla_ba": gla_b_a[0:1],
        "gla_gn": gla_norm_g[0:1],
        "gla_w_out": gla_w_out[0],
        "conv_w_in": conv_w_in[0],
        "conv_b_in": conv_b_in[0:1],
        "conv_w": conv_w[0],
        "conv_b": conv_b[0:1],
        "conv_ln_g": conv_ln_g[0:1],
        "conv_ln_b": conv_ln_b[0:1],
        "conv_w_out": conv_w_out[0],
    }


def kernel(x_prompt, x_sample, state_gla, state_conv, norm_g, final_norm_g, gla_w_in, gla_w_a2, gla_b_a,
           gla_norm_g, gla_w_out, conv_w_in, conv_b_in, conv_w, conv_b, conv_ln_g, conv_ln_b, conv_w_out):
    p = _prepare(norm_g, final_norm_g, gla_w_in, gla_w_a2, gla_b_a, gla_norm_g, gla_w_out, conv_w_in, conv_b_in,
                 conv_w, conv_b, conv_ln_g, conv_ln_b, conv_w_out)
    conv_hist = jnp.pad(state_conv[0], ((0, 0), (HALO - CONV_PAD, 0), (0, 0)))
    return _forward(x_prompt, x_sample, state_gla[0], conv_hist, p)
```

```python
import functools

import jax
import jax.numpy as jnp
from jax import lax
from jax.experimental import pallas as pl
from jax.experimental.pallas import tpu as pltpu

F32 = jnp.float32
BF16 = jnp.bfloat16

D_MODEL = 2048
CHUNK = 64
GLA_HEADS = 4
GLA_DK = 1024
GLA_DV = 2048
DK_HEAD = GLA_DK // GLA_HEADS
DV_HEAD = GLA_DV // GLA_HEADS
GATE_RANK = 16
GATE_NORM = 16.0
GLA_MAIN = 2 * GLA_DK + 2 * GLA_DV
CONV_DIM = 2048
CONV_WIDTH = 31
CONV_PAD = CONV_WIDTH - 1
EPS = 1e-6
LOG2E = 1.4426950408889634

SUBLANES = 8
LANES = 128
HALO = 32
ROW_STRIDE = 2
TM = 256
TM_OUT = 512
W_CHUNK = 256
W_SLOTS = 4
W_ROWS_IN = 32
W_ROWS_OUT = 128
W_ROWS_OUT1 = 64
SEG = 64
VMEM_LIMIT = 56 * 1024 * 1024


def _cparams(sem):
    return pltpu.CompilerParams(dimension_semantics=sem, vmem_limit_bytes=VMEM_LIMIT)


def _after(x, dep):
    half = jnp.uint32(16)
    zero = lax.shift_right_logical(lax.shift_right_logical(lax.bitcast_convert_type(dep, jnp.uint32), half), half)
    return lax.bitcast_convert_type(lax.bitcast_convert_type(x, jnp.uint32) | zero, F32)


def _silu(x):
    return x * jax.nn.sigmoid(x)


def _rmsnorm_rows(x, g):
    ms = jnp.mean(x * x, axis=-1, keepdims=True)
    return x * lax.rsqrt(ms + EPS) * g


def _stream_rows_bf16(w_hbm, w_scr, stage, sem, rows):
    n_slots = stage.shape[0]
    n_w = w_scr.shape[0] // rows

    def chunk(j):
        slot = j % n_slots
        return pltpu.make_async_copy(w_hbm.at[pl.ds(j * rows, rows), :], stage.at[slot], sem.at[slot])

    for j in range(n_slots - 1):
        chunk(j).start()
    for j in range(n_w):
        if j + n_slots - 1 < n_w:
            chunk(j + n_slots - 1).start()
        chunk(j).wait()
        w_scr[j * rows:(j + 1) * rows, :] = stage[j % n_slots].astype(BF16)


def _resident(shape):
    return pl.BlockSpec(shape, lambda *_: (0,) * len(shape), pipeline_mode=pl.Buffered(1))


def _tile_spec(width, lag, first, count, tm=TM):
    return pl.BlockSpec((tm, width), lambda s: (jnp.clip(s - lag - first, 0, count - 1), 0))


def _gla_front_kernel(xp_ref, xs_ref, g_ref, w_hbm, wa2_ref, ba_ref, o_ref, la_ref, h_cur, h_nxt,
                      w_ref, wa1t_ref, stage, sem, *, n_p, tn):
    s = pl.program_id(0)

    def weight_chunk(j):
        slot = j % W_SLOTS
        return pltpu.make_async_copy(w_hbm.at[pl.ds(j * W_CHUNK, W_CHUNK), :], stage.at[slot], sem.at[slot])

    @pl.when(s == 0)
    def _():
        h_nxt[...] = jnp.zeros_like(h_nxt)
        n_w = GLA_MAIN // W_CHUNK
        for j in range(W_SLOTS - 1):
            weight_chunk(j).start()
        for j in range(n_w):
            if j + W_SLOTS - 1 < n_w:
                weight_chunk(j + W_SLOTS - 1).start()
            weight_chunk(j).wait()
            w_ref[:, j * W_CHUNK:(j + 1) * W_CHUNK] = jnp.transpose(stage[j % W_SLOTS]).astype(BF16)
        gate_rows = pltpu.make_async_copy(w_hbm.at[pl.ds(GLA_MAIN, GATE_RANK), :],
                                          stage.at[0, pl.ds(0, GATE_RANK), :], sem.at[0])
        gate_rows.start()
        gate_rows.wait()
        wa1t_ref[...] = jnp.zeros_like(wa1t_ref)
        wa1t_ref[0:GATE_RANK, :] = stage[0, 0:GATE_RANK, :].astype(BF16)

    h_cur[...] = h_nxt[...]

    h = h_cur[...]
    a1 = lax.dot_general(h, wa1t_ref[...], (((1,), (1,)), ((), ())), preferred_element_type=F32)
    z = jnp.dot(a1.astype(BF16), wa2_ref[...], preferred_element_type=F32) + ba_ref[...]
    la_ref[...] = (jnp.minimum(z, 0.0) - jnp.log(1.0 + jnp.exp(-jnp.abs(z)))) * (1.0 / GATE_NORM)
    for n0 in range(0, GLA_MAIN, tn):
        cols = slice(n0, n0 + tn)
        o_ref[:, cols] = jnp.dot(h_cur[...], w_ref[:, cols], preferred_element_type=F32).astype(BF16)

    x = jnp.where(s < n_p, xp_ref[...], xs_ref[...])
    h_nxt[...] = _rmsnorm_rows(x, g_ref[...]).astype(BF16)


def _gla_front(xp, xs, g, w_in_f32, wa2p, ba, tn=512):
    n_p, n_s = xp.shape[0] // TM, xs.shape[0] // TM
    n = n_p + n_s
    return pl.pallas_call(
        functools.partial(_gla_front_kernel, n_p=n_p, tn=tn),
        grid=(n + 1,),
        in_specs=[
            _tile_spec(D_MODEL, 0, 0, n_p),
            _tile_spec(D_MODEL, 0, n_p, n_s),
            _resident((1, D_MODEL)),
            pl.BlockSpec(memory_space=pl.ANY),
            _resident((LANES, GLA_DK)),
            _resident((1, GLA_DK)),
        ],
        out_specs=[_tile_spec(GLA_MAIN, 1, 0, n), _tile_spec(GLA_DK, 1, 0, n)],
        out_shape=[
            jax.ShapeDtypeStruct((n * TM, GLA_MAIN), BF16),
            jax.ShapeDtypeStruct((n * TM, GLA_DK), F32),
        ],
        scratch_shapes=[
            pltpu.VMEM((TM, D_MODEL), BF16),
            pltpu.VMEM((TM, D_MODEL), BF16),
            pltpu.VMEM((D_MODEL, GLA_MAIN), BF16),
            pltpu.VMEM((LANES, D_MODEL), BF16),
            pltpu.VMEM((W_SLOTS, W_CHUNK, D_MODEL), F32),
            pltpu.SemaphoreType.DMA((W_SLOTS,)),
        ],
        compiler_params=_cparams(("arbitrary",)),
        name="gla_front",
    )(xp, xs, g, w_in_f32, wa2p, ba)


def _chunk_cumsum(la_ref, r0):
    sub = lax.broadcasted_iota(jnp.int32, (SUBLANES, la_ref.shape[1]), 0)
    groups = []
    carry = None
    for gi in range(CHUNK // SUBLANES):
        x = la_ref[pl.ds(r0 + gi * SUBLANES, SUBLANES), :]
        for s in (1, 2, 4):
            x = x + jnp.where(sub >= s, pltpu.roll(x, s, 0), 0.0)
        if carry is not None:
            x = x + carry
        carry = x[SUBLANES - 1:SUBLANES, :]
        groups.append(x)
    return jnp.concatenate(groups, axis=0)


def _gla_core_kernel(*refs, n_chunks, has_state):
    if has_state:
        q_ref, k_ref, v_ref, r_ref, la_ref, gn_ref, s0_ref, og_ref, so_ref, s_ref = refs
    else:
        q_ref, k_ref, v_ref, r_ref, la_ref, gn_ref, og_ref, so_ref, s_ref = refs
    t = pl.program_id(2)

    @pl.when(t == 0)
    def _():
        if has_state:
            s_ref[...] = s0_ref[0, 0]
        else:
            s_ref[...] = jnp.zeros_like(s_ref)

    row = lax.broadcasted_iota(jnp.int32, (CHUNK, CHUNK), 0)
    col = lax.broadcasted_iota(jnp.int32, (CHUNK, CHUNK), 1)
    causal = row >= col

    def chunk_body(c, carry):
        r0 = pl.multiple_of(c * CHUNK, CHUNK)
        rows = pl.ds(r0, CHUNK)
        b2 = _chunk_cumsum(la_ref, r0) * LOG2E
        bl2 = b2[CHUNK - 1:CHUNK, :]
        q = q_ref[rows, :].astype(F32)
        k = k_ref[rows, :].astype(F32)
        qh = (q * (jnp.exp2(b2) * (DK_HEAD ** -0.5))).astype(BF16)
        ke = (k * jnp.exp2(-b2)).astype(BF16)
        kd_t = jnp.transpose(k * jnp.exp2(bl2 - b2)).astype(BF16)
        dec_t = jnp.transpose(jnp.broadcast_to(jnp.exp2(bl2), (SUBLANES, DK_HEAD)))
        att = lax.dot_general(qh, ke, (((1,), (1,)), ((), ())), preferred_element_type=F32)
        att = jnp.where(causal, att, 0.0).astype(BF16)
        vh = v_ref[rows, :]
        s_old = s_ref[...]
        o = jnp.dot(qh, s_old.astype(BF16), preferred_element_type=F32) + jnp.dot(
            att, vh, preferred_element_type=F32)
        s_ref[...] = dec_t[:, 0:1] * s_old + jnp.dot(kd_t, vh, preferred_element_type=F32)
        on = _rmsnorm_rows(o, gn_ref[...])
        gate = _silu(r_ref[rows, :].astype(F32))
        og_ref[rows, :] = (on * gate).astype(BF16)
        return carry

    lax.fori_loop(0, n_chunks, chunk_body, 0, unroll=8 if n_chunks % 8 == 0 else 1)

    @pl.when(t == pl.num_programs(2) - 1)
    def _():
        so_ref[0, 0] = s_ref[...]


def _gla_core(qkvr, loga, gnorm, s0, n_seq, seq_len, row0, tt):
    has_state = s0 is not None
    tps = seq_len // tt
    blk0 = row0 // tt

    def rows(width, first_blk):
        return pl.BlockSpec((tt, width), lambda b, h, t: (blk0 + b * tps + t, first_blk + h))

    k_blk = GLA_DK // DK_HEAD
    v_blk = 2 * GLA_DK // DV_HEAD
    r_blk = v_blk + GLA_DV // DV_HEAD
    in_specs = [rows(DK_HEAD, 0), rows(DK_HEAD, k_blk), rows(DV_HEAD, v_blk), rows(DV_HEAD, r_blk),
                rows(DK_HEAD, 0), pl.BlockSpec((1, DV_HEAD), lambda b, h, t: (0, h))]
    args = [qkvr, qkvr, qkvr, qkvr, loga, gnorm]
    state_spec = pl.BlockSpec((1, 1, DK_HEAD, DV_HEAD), lambda b, h, t: (b, h, 0, 0))
    if has_state:
        in_specs.append(state_spec)
        args.append(s0)
    return pl.pallas_call(
        functools.partial(_gla_core_kernel, n_chunks=tt // CHUNK, has_state=has_state),
        grid=(n_seq, GLA_HEADS, tps),
        in_specs=in_specs,
        out_specs=[pl.BlockSpec((tt, DV_HEAD), lambda b, h, t: (b * tps + t, h)), state_spec],
        out_shape=[
            jax.ShapeDtypeStruct((n_seq * seq_len, GLA_DV), BF16),
            jax.ShapeDtypeStruct((n_seq, GLA_HEADS, DK_HEAD, DV_HEAD), F32),
        ],
        scratch_shapes=[pltpu.VMEM((DK_HEAD, DV_HEAD), F32)],
        compiler_params=_cparams(("arbitrary", "arbitrary", "arbitrary")),
        name="gla_core",
    )(*args)


def _outproj0_kernel(ap_ref, as_ref, w_hbm, xp_ref, xs_ref, o_ref, w_ref, stage, sem, *, n_p, tn):
    @pl.when(pl.program_id(0) == 0)
    def _():
        _stream_rows_bf16(w_hbm, w_ref, stage, sem, W_ROWS_OUT)

    is_p = pl.program_id(0) < n_p
    a = jnp.where(is_p, ap_ref[...], as_ref[...])
    for n0 in range(0, D_MODEL, tn):
        cols = slice(n0, n0 + tn)
        x = jnp.where(is_p, xp_ref[:, cols], xs_ref[:, cols])
        o_ref[:, cols] = x + jnp.dot(a, w_ref[:, cols], preferred_element_type=F32)


def _outproj0(og_p, og_s, w, xp, xs, tn=512):
    n_p, n_s = xp.shape[0] // TM_OUT, xs.shape[0] // TM_OUT
    n = n_p + n_s
    rows = functools.partial(_tile_spec, tm=TM_OUT)
    return pl.pallas_call(
        functools.partial(_outproj0_kernel, n_p=n_p, tn=tn),
        grid=(n,),
        in_specs=[
            rows(GLA_DV, 0, 0, n_p),
            rows(GLA_DV, 0, n_p, n_s),
            pl.BlockSpec(memory_space=pl.ANY),
            rows(D_MODEL, 0, 0, n_p),
            rows(D_MODEL, 0, n_p, n_s),
        ],
        out_specs=rows(D_MODEL, 0, 0, n),
        out_shape=jax.ShapeDtypeStruct((n * TM_OUT, D_MODEL), F32),
        scratch_shapes=[
            pltpu.VMEM((GLA_DV, D_MODEL), BF16),
            pltpu.VMEM((W_SLOTS, W_ROWS_OUT, D_MODEL), F32),
            pltpu.SemaphoreType.DMA((W_SLOTS,)),
        ],
        compiler_params=_cparams(("arbitrary",)),
        name="outproj0",
    )(og_p, og_s, w, xp, xs)


def _time_rows(start, n):
    return pl.ds(ROW_STRIDE * start, n, stride=ROW_STRIDE)


def _conv_front_kernel(x_ref, g_ref, w_hbm, b_ref, cw_ref, cb_ref, st_ref, sz_ref, c_ref, sop_ref, sos_ref,
                       h_cur, h_nxt, u_stage, buf, hist, w_ref, stage, sem, *, n_p, tiles_per_seq, tn):
    s = pl.program_id(0)
    nseg = TM // SEG
    region = HALO + SEG
    n_slab = CONV_DIM // LANES
    first = HALO - CONV_PAD

    @pl.when(s == 0)
    def _():
        h_nxt[...] = jnp.zeros_like(h_nxt)
        u_stage[...] = jnp.zeros_like(u_stage)
        hist[...] = jnp.zeros_like(hist)
        _stream_rows_bf16(w_hbm, w_ref, stage, sem, W_ROWS_IN)

    h_cur[...] = h_nxt[...]
    t_conv = jnp.maximum(s - 2, 0)
    is_p = t_conv < n_p
    for g in range(nseg):
        for sl in range(n_slab):
            lanes = slice(sl * LANES, (sl + 1) * LANES)
            buf[sl, _time_rows(g * region + HALO, SEG), :] = u_stage[g * SEG:(g + 1) * SEG, lanes]

    @pl.when(is_p)
    def _():
        @pl.when(lax.rem(t_conv, tiles_per_seq) == 0)
        def _():
            hist[...] = jnp.zeros_like(hist)

        for sl in range(n_slab):
            lanes = slice(sl * LANES, (sl + 1) * LANES)
            buf[sl, _time_rows(0, HALO), :] = hist[:, lanes]
            for g in range(1, nseg):
                buf[sl, _time_rows(g * region, HALO), :] = u_stage[g * SEG - HALO:g * SEG, lanes]
        tail = u_stage[TM - HALO:TM, :]
        hist[...] = tail

        @pl.when(lax.rem(t_conv, tiles_per_seq) == tiles_per_seq - 1)
        def _():
            sop_ref[t_conv // tiles_per_seq] = tail

    @pl.when(jnp.logical_not(is_p))
    def _():
        for g in range(nseg):
            seq_i = (t_conv - n_p) * nseg + g
            for sl in range(n_slab):
                lanes = slice(sl * LANES, (sl + 1) * LANES)
                buf[sl, _time_rows(g * region, HALO), :] = st_ref[seq_i, :, lanes]
            sos_ref[seq_i] = u_stage[(g + 1) * SEG - HALO:(g + 1) * SEG, :]

    n_chunks = CONV_DIM // tn
    slabs_per_chunk = n_slab // n_chunks
    prev_acc = None
    for j in range(n_chunks):
        n0 = j * tn
        cols = slice(n0, n0 + tn)
        gcols = slice(CONV_DIM + n0, CONV_DIM + n0 + tn)
        zcols = slice(2 * CONV_DIM + n0, 2 * CONV_DIM + n0 + tn)
        a = jnp.dot(h_cur[...], w_ref[:, cols], preferred_element_type=F32) + b_ref[:, cols]
        ga = jnp.dot(h_cur[...], w_ref[:, gcols], preferred_element_type=F32) + b_ref[:, gcols]
        u_stage[:, cols] = a * jax.nn.sigmoid(ga)
        z = jnp.dot(h_cur[...], w_ref[:, zcols], preferred_element_type=F32) + b_ref[:, zcols]
        sz_ref[:, cols] = _silu(z).astype(BF16)

        for sl in range(j * slabs_per_chunk, (j + 1) * slabs_per_chunk):
            lanes = slice(sl * LANES, (sl + 1) * LANES)
            for g in range(nseg):
                acc = jnp.broadcast_to(cb_ref[:, lanes], (SEG, LANES))
                if prev_acc is not None:
                    acc = _after(acc, prev_acc)
                for kk in range(CONV_WIDTH):
                    acc = acc + buf[sl, _time_rows(g * region + first + kk, SEG), :] * cw_ref[kk:kk + 1, lanes]
                c_ref[g * SEG:(g + 1) * SEG, lanes] = acc
                prev_acc = acc

    h_nxt[...] = _rmsnorm_rows(x_ref[...], g_ref[...]).astype(BF16)


def _conv_front(x1, g, w, b, cw, cb, st, n_p, n_prompt_seq, tn=1024):
    n = x1.shape[0] // TM
    n_stream = st.shape[0]
    return pl.pallas_call(
        functools.partial(_conv_front_kernel, n_p=n_p, tiles_per_seq=n_p // n_prompt_seq, tn=tn),
        grid=(n + 2,),
        in_specs=[
            _tile_spec(D_MODEL, 0, 0, n),
            _resident((1, D_MODEL)),
            pl.BlockSpec(memory_space=pl.ANY),
            _resident((1, 3 * CONV_DIM)),
            _resident((CONV_WIDTH, CONV_DIM)),
            _resident((1, CONV_DIM)),
            _resident((n_stream, HALO, CONV_DIM)),
        ],
        out_specs=[
            _tile_spec(CONV_DIM, 1, 0, n),
            _tile_spec(CONV_DIM, 2, 0, n),
            pl.BlockSpec((n_prompt_seq, HALO, CONV_DIM), lambda s: (0, 0, 0)),
            pl.BlockSpec((n_stream, HALO, CONV_DIM), lambda s: (0, 0, 0)),
        ],
        out_shape=[
            jax.ShapeDtypeStruct((n * TM, CONV_DIM), BF16),
            jax.ShapeDtypeStruct((n * TM, CONV_DIM), F32),
            jax.ShapeDtypeStruct((n_prompt_seq, HALO, CONV_DIM), F32),
            jax.ShapeDtypeStruct((n_stream, HALO, CONV_DIM), F32),
        ],
        scratch_shapes=[
            pltpu.VMEM((TM, D_MODEL), BF16),
            pltpu.VMEM((TM, D_MODEL), BF16),
            pltpu.VMEM((TM, CONV_DIM), F32),
            pltpu.VMEM((CONV_DIM // LANES, ROW_STRIDE * (TM // SEG) * (HALO + SEG), LANES), F32),
            pltpu.VMEM((HALO, CONV_DIM), F32),
            pltpu.VMEM((D_MODEL, 3 * CONV_DIM), BF16),
            pltpu.VMEM((W_SLOTS, W_ROWS_IN, 3 * CONV_DIM), F32),
            pltpu.SemaphoreType.DMA((W_SLOTS,)),
        ],
        compiler_params=_cparams(("arbitrary",)),
        name="conv_front",
    )(x1, g, w, b, cw, cb, st)


def _outproj1_kernel(c_ref, sz_ref, lg_ref, lb_ref, w_hbm, x_ref, fg_ref, yp_ref, ys_ref, a_scr, y_scr,
                     w_ref, stage, sem, *, n_p, tn):
    @pl.when(pl.program_id(0) == 0)
    def _():
        _stream_rows_bf16(w_hbm, w_ref, stage, sem, W_ROWS_OUT1)

    c = c_ref[...]
    mu = jnp.mean(c, axis=-1, keepdims=True)
    xc = c - mu
    ln = xc * lax.rsqrt(jnp.mean(xc * xc, axis=-1, keepdims=True) + EPS) * lg_ref[...] + lb_ref[...]
    a_scr[...] = (_silu(ln) * sz_ref[...].astype(F32)).astype(BF16)

    for n0 in range(0, D_MODEL, tn):
        cols = slice(n0, n0 + tn)
        y_scr[:, cols] = x_ref[:, cols] + jnp.dot(a_scr[...], w_ref[:, cols], preferred_element_type=F32)
    y = y_scr[...]
    scale = lax.rsqrt(jnp.mean(y * y, axis=-1, keepdims=True) + EPS)
    is_p = pl.program_id(0) < n_p

    @pl.when(is_p)
    def _():
        yp_ref[...] = y_scr[...] * scale * fg_ref[...]

    @pl.when(jnp.logical_not(is_p))
    def _():
        ys_ref[...] = y_scr[...] * scale * fg_ref[...]


def _outproj1(c, sz, lg, lb, w, x1, fg, n_p_rows, tn=512):
    n = x1.shape[0] // TM_OUT
    n_p = n_p_rows // TM_OUT
    n_s = n - n_p
    vec = _resident((1, D_MODEL))
    rows = functools.partial(_tile_spec, tm=TM_OUT)
    return pl.pallas_call(
        functools.partial(_outproj1_kernel, n_p=n_p, tn=tn),
        grid=(n,),
        in_specs=[
            rows(CONV_DIM, 0, 0, n),
            rows(CONV_DIM, 0, 0, n),
            vec, vec,
            pl.BlockSpec(memory_space=pl.ANY),
            rows(D_MODEL, 0, 0, n),
            vec,
        ],
        out_specs=[rows(D_MODEL, 0, 0, n_p), rows(D_MODEL, 0, n_p, n_s)],
        out_shape=[
            jax.ShapeDtypeStruct((n_p * TM_OUT, D_MODEL), F32),
            jax.ShapeDtypeStruct((n_s * TM_OUT, D_MODEL), F32),
        ],
        scratch_shapes=[
            pltpu.VMEM((TM_OUT, CONV_DIM), BF16),
            pltpu.VMEM((TM_OUT, D_MODEL), F32),
            pltpu.VMEM((CONV_DIM, D_MODEL), BF16),
            pltpu.VMEM((W_SLOTS, W_ROWS_OUT1, D_MODEL), F32),
            pltpu.SemaphoreType.DMA((W_SLOTS,)),
        ],
        compiler_params=_cparams(("arbitrary",)),
        name="outproj1",
    )(c, sz, lg, lb, w, x1, fg)


def _forward(x_prompt, x_sample, gla_state, conv_hist, p, tt_prompt=4096):
    n_pseq, p_len, _ = x_prompt.shape
    n_sseq, s_len, _ = x_sample.shape
    assert s_len == SEG and p_len % TM == 0 and (n_sseq * s_len) % TM == 0
    xp = x_prompt.reshape(n_pseq * p_len, D_MODEL)
    xs = x_sample.reshape(n_sseq * s_len, D_MODEL)
    n_p = xp.shape[0] // TM

    qkvr, loga = _gla_front(xp, xs, p["norm_g0"], p["gla_w_in_t"], p["gla_wa2"], p["gla_ba"])
    og_p, gla_p = _gla_core(qkvr, loga, p["gla_gn"], None, n_pseq, p_len, 0, tt_prompt)
    og_s, gla_s = _gla_core(qkvr, loga, p["gla_gn"], gla_state, n_sseq, s_len, xp.shape[0], s_len)
    x1 = _outproj0(og_p, og_s, p["gla_w_out"], xp, xs)

    sz, c, conv_p, conv_s = _conv_front(x1, p["norm_g1"], p["conv_w_in"], p["conv_b_in"], p["conv_w"],
                                        p["conv_b"], conv_hist, n_p, n_pseq)
    y_p, y_s = _outproj1(c, sz, p["conv_ln_g"], p["conv_ln_b"], p["conv_w_out"], x1, p["final_g"], xp.shape[0])
    keep = slice(HALO - CONV_PAD, HALO)
    return (y_p.reshape(x_prompt.shape), y_s.reshape(x_sample.shape), gla_p[None], conv_p[None, :, keep, :],
            gla_s[None], conv_s[None, :, keep, :])


def _prepare(norm_g, final_norm_g, gla_w_in, gla_w_a2, gla_b_a, gla_norm_g, gla_w_out, conv_w_in, conv_b_in,
             conv_w, conv_b, conv_ln_g, conv_ln_b, conv_w_out):
    w_in0 = gla_w_in[0]
    return {
        "norm_g0": norm_g[0:1],
        "norm_g1": norm_g[1:2],
        "final_g": final_norm_g[None, :],
        "gla_w_in_t": jnp.swapaxes(w_in0, 0, 1),
        "gla_wa2": jnp.pad(gla_w_a2[0], ((0, LANES - GATE_RANK), (0, 0))).astype(BF16),
        "gla_ba": gla_b_a[0:1],
        "gla_gn": gla_norm_g[0:1],
        "gla_w_out": gla_w_out[0],
        "conv_w_in": conv_w_in[0],
        "conv_b_in": conv_b_in[0:1],
        "conv_w": conv_w[0],
        "conv_b": conv_b[0:1],
        "conv_ln_g": conv_ln_g[0:1],
        "conv_ln_b": conv_ln_b[0:1],
        "conv_w_out": conv_w_out[0],
    }


def kernel(x_prompt, x_sample, state_gla, state_conv, norm_g, final_norm_g, gla_w_in, gla_w_a2, gla_b_a,
           gla_norm_g, gla_w_out, conv_w_in, conv_b_in, conv_w, conv_b, conv_ln_g, conv_ln_b, conv_w_out):
    p = _prepare(norm_g, final_norm_g, gla_w_in, gla_w_a2, gla_b_a, gla_norm_g, gla_w_out, conv_w_in, conv_b_in,
                 conv_w, conv_b, conv_ln_g, conv_ln_b, conv_w_out)
    conv_hist = jnp.pad(state_conv[0], ((0, 0), (HALO - CONV_PAD, 0), (0, 0)))
    return _forward(x_prompt, x_sample, state_gla[0], conv_hist, p)
```
